```python
import math
import jax, jax.numpy as jnp
from jax import lax
import numpy as np

D_MODEL = 4096
BATCH = 4
SEQ = 2048
DEPTH = 4
DEC_BATCH = 8
DEC_SEQ = 1
PAST_LEN = 8192
PAGE_SIZE = 128

N_MIXERS = 4
HEAD_DIM = 128
MIX_WIDTH = D_MODEL // N_MIXERS
N_HEADS = MIX_WIDTH // HEAD_DIM
MOBA_BLOCK = 256
MOBA_TOPK = 3
DIFF_KV_HEADS = N_HEADS // 2
DIFF_REP = N_HEADS // DIFF_KV_HEADS
DIFF_SUB = HEAD_DIM // 2
NSA_GROUPS = 2
NSA_REP = N_HEADS // NSA_GROUPS
NSA_BLOCK = 64
NSA_N_SEL = 16
NSA_WINDOW = 512
NSA_CMP_HIDDEN = 256
D_FF = ((8 * D_MODEL // 3 + 255) // 256) * 256
ROPE_THETA = 10000.0
EPS = 1e-6
NEG_INF = -1e30
Q_BLOCK = 128
GATHER_Q_BLOCK = 32

NSA_KV_W = NSA_GROUPS * HEAD_DIM
DIFF_KV_W = DIFF_KV_HEADS * HEAD_DIM
IN_SPLITS = (MIX_WIDTH, MIX_WIDTH, MIX_WIDTH,
             MIX_WIDTH, DIFF_KV_W, DIFF_KV_W,
             MIX_WIDTH, MIX_WIDTH, MIX_WIDTH,
             MIX_WIDTH, NSA_KV_W, NSA_KV_W, NSA_KV_W, NSA_KV_W, NSA_KV_W, NSA_KV_W,
             N_HEADS * 3,
             N_MIXERS * D_MODEL)
N_IN = int(sum(IN_SPLITS))
IN_OFFSETS = tuple(int(o) for o in np.cumsum(IN_SPLITS)[:-1])

kernel_name = 'hybrid_moba_diff_stickbreak_nsa_decoder_step'


def rmsnorm(x, g):
    x32 = x.astype(jnp.float32)
    y = x32 * lax.rsqrt(jnp.mean(x32 * x32, axis=-1, keepdims=True) + EPS)
    return (y * g.astype(jnp.float32)).astype(x.dtype)


def rope(x, pos):
    half = x.shape[-1] // 2
    inv = ROPE_THETA ** (-jnp.arange(half, dtype=jnp.float32) / half)
    ang = pos[:, None] * inv[None, :]
    shape = (1, x.shape[1]) + (1,) * (x.ndim - 3) + (half,)
    cos, sin = jnp.cos(ang).reshape(shape), jnp.sin(ang).reshape(shape)
    x32 = x.astype(jnp.float32)
    x1, x2 = x32[..., :half], x32[..., half:]
    return jnp.concatenate([x1 * cos - x2 * sin, x2 * cos + x1 * sin], axis=-1).astype(x.dtype)


def masked_softmax(s, mask):
    s = jnp.where(mask, s.astype(jnp.float32), NEG_INF)
    return jnp.where(mask, jax.nn.softmax(s, axis=-1), 0.0)


def swiglu(x, w_gu, w_dn):
    g, u = jnp.split(x @ w_gu, 2, axis=-1)
    return (jax.nn.silu(g) * u) @ w_dn


def sweep(fn, n_q, qb, *xs):
    if n_q <= qb or n_q % qb:
        return fn(jnp.int32(0), *xs)
    n = n_q // qb
    blk = tuple(jnp.moveaxis(a.reshape((a.shape[0], n, qb) + a.shape[2:]), 1, 0) for a in xs)
    starts = jnp.arange(n, dtype=jnp.int32) * qb
    out = lax.map(lambda a: fn(a[0], *a[1:]), (starts,) + blk)
    out = jnp.moveaxis(out, 0, 1)
    return out.reshape((out.shape[0], n_q) + out.shape[3:])


def gather_pages(pool, page_table):
    g = pool[page_table]
    return g.reshape((g.shape[0], g.shape[1] * g.shape[2]) + g.shape[3:])


def moba_attend(q, kv, P):
    B, S, H, dh = q.shape
    L = kv.shape[1]
    nb = -(-L // MOBA_BLOCK)
    kvb = jnp.pad(kv, ((0, 0), (0, nb * MOBA_BLOCK - L), (0, 0), (0, 0), (0, 0)))
    kvb = kvb.reshape(B, nb, MOBA_BLOCK, 2, H, dh).transpose(3, 0, 4, 1, 2, 5)
    kb, vb = kvb[0], kvb[1]
    kmean = jnp.mean(kb.astype(jnp.float32), axis=3)
    n_top = min(MOBA_TOPK, nb)
    M = n_top + 1
    bi = jnp.arange(B)[:, None, None, None]
    hi = jnp.arange(H)[None, None, :, None]
    blk_off = jnp.arange(MOBA_BLOCK)
    scale = dh ** -0.5

    def block(i0, qb):
        QB = qb.shape[1]
        pos = P + i0 + jnp.arange(QB)
        own = pos // MOBA_BLOCK
        gate = jnp.einsum('bqhd,bhnd->bqhn', qb.astype(jnp.float32), kmean)
        cand = (jnp.arange(nb)[None, :] < own[:, None])[None, :, None, :]
        top_v, top_i = lax.top_k(jnp.where(cand, gate, -jnp.inf), n_top)
        sel = jnp.concatenate([top_i, jnp.broadcast_to(own[None, :, None, None], (B, QB, H, 1))], axis=-1)
        ok = jnp.concatenate([top_v > -jnp.inf, jnp.ones((B, QB, H, 1), bool)], axis=-1)
        kg = kb[bi, hi, sel].reshape(B, QB, H, M * MOBA_BLOCK, dh)
        vg = vb[bi, hi, sel].reshape(B, QB, H, M * MOBA_BLOCK, dh)
        kpos = (sel[..., None] * MOBA_BLOCK + blk_off).reshape(B, QB, H, M * MOBA_BLOCK)
        mask = jnp.repeat(ok, MOBA_BLOCK, axis=-1) & (kpos <= pos[None, :, None, None])
        s = jnp.einsum('bqhd,bqhkd->bqhk', qb, kg, preferred_element_type=jnp.float32) * scale
        p = masked_softmax(s, mask)
        return jnp.einsum('bqhk,bqhkd->bqhd', p.astype(vg.dtype), vg)

    return sweep(block, S, GATHER_Q_BLOCK, q).reshape(B, S, H * dh)


def diff_attend(q, kv, P, lam, lam_init, subln_g):
    B, S, G, R, _, ds = q.shape
    L = kv.shape[1]
    k = kv[:, :, 0].reshape(B, L, G, 2, ds)
    v = kv[:, :, 1]
    kpos = jnp.arange(L)
    scale = ds ** -0.5

    def block(i0, qb):
        pos = P + i0 + jnp.arange(qb.shape[1])
        s = jnp.einsum('bqgrcd,bkgcd->bgrcqk', qb, k, preferred_element_type=jnp.float32) * scale
        p = masked_softmax(s, kpos[None, :] <= pos[:, None])
        a = p[:, :, :, 0] - lam * p[:, :, :, 1]
        return jnp.einsum('bgrqk,bkgd->bqgrd', a.astype(v.dtype), v)

    o = sweep(block, S, Q_BLOCK, q)
    return (rmsnorm(o, subln_g) * (1.0 - lam_init)).reshape(B, S, G * R * v.shape[-1])


def sb_attend(q, kv, P):
    B, S, H, dh = q.shape
    k, v = kv[:, :, 0], kv[:, :, 1]
    kpos = jnp.arange(kv.shape[1])
    scale = dh ** -0.5

    def block(i0, qb):
        pos = P + i0 + jnp.arange(qb.shape[1])
        z = jnp.einsum('bqhd,bkhd->bhqk', qb, k, preferred_element_type=jnp.float32) * scale
        past = kpos[None, :] < pos[:, None]
        log_keep = jnp.where(past, jax.nn.log_sigmoid(-z), 0.0)
        log_after = lax.cumsum(log_keep, axis=3, reverse=True) - log_keep
        a = jnp.where(past, jnp.exp(jax.nn.log_sigmoid(z) + log_after), 0.0)
        return jnp.einsum('bhqk,bkhd->bqhd', a.astype(v.dtype), v)

    return sweep(block, S, Q_BLOCK, q).reshape(B, S, H * dh)


def nsa_compress(x, pos_emb, w1, w2):
    B, L, G, dh = x.shape
    nc = -(-L // NSA_BLOCK)
    xb = jnp.pad(x, ((0, 0), (0, nc * NSA_BLOCK - L), (0, 0), (0, 0))).reshape(B, nc, NSA_BLOCK, G, dh)
    xb = (xb + pos_emb[:, None, :]).transpose(0, 1, 3, 2, 4).reshape(B, nc, G, NSA_BLOCK * dh)
    return jax.nn.gelu(xb @ w1) @ w2


def nsa_attend(q, cmp_kv, sel_kv, win_kv, gates, P, w_start, cmp_pos, cmp_w1, cmp_w2):
    B, S, G, R, dh = q.shape
    kc = nsa_compress(cmp_kv[:, :, 0], cmp_pos[0], cmp_w1[0], cmp_w2[0])
    vc = nsa_compress(cmp_kv[:, :, 1], cmp_pos[1], cmp_w1[1], cmp_w2[1])
    nc = kc.shape[1]
    L = sel_kv.shape[1]
    selb = jnp.pad(sel_kv, ((0, 0), (0, nc * NSA_BLOCK - L), (0, 0), (0, 0), (0, 0)))
    selb = selb.reshape(B, nc, NSA_BLOCK, 2, G, dh).transpose(3, 0, 4, 1, 2, 5)
    ksb, vsb = selb[0], selb[1]
    winp = jnp.pad(win_kv, ((0, 0), (NSA_WINDOW, 0), (0, 0), (0, 0), (0, 0)))
    n_top = min(NSA_N_SEL - 1, nc)
    M = n_top + 1
    bi = jnp.arange(B)[:, None, None, None]
    gi = jnp.arange(G)[None, :, None, None]
    blk_off = jnp.arange(NSA_BLOCK)
    cidx = jnp.arange(nc)
    scale = dh ** -0.5

    def block(i0, qb):
        QB = qb.shape[1]
        pos = P + i0 + jnp.arange(QB)
        s_c = jnp.einsum('bqgrd,bngd->bgrqn', qb, kc, preferred_element_type=jnp.float32) * scale
        vis = (cidx[None, :] + 1) * NSA_BLOCK <= pos[:, None] + 1
        p_c = masked_softmax(s_c, vis)
        o_c = jnp.einsum('bgrqn,bngd->bqgrd', p_c.astype(vc.dtype), vc)
        own = pos // NSA_BLOCK
        imp = jnp.where(cidx[None, :] < own[:, None], jnp.sum(p_c, axis=2), -jnp.inf)
        top_v, top_i = lax.top_k(imp, n_top)
        sel = jnp.concatenate([top_i, jnp.broadcast_to(own[None, None, :, None], (B, G, QB, 1))], axis=-1)
        ok = jnp.concatenate([top_v > -jnp.inf, jnp.ones((B, G, QB, 1), bool)], axis=-1)
        kg = ksb[bi, gi, sel].reshape(B, G, QB, M * NSA_BLOCK, dh)
        vg = vsb[bi, gi, sel].reshape(B, G, QB, M * NSA_BLOCK, dh)
        kpos = (sel[..., None] * NSA_BLOCK + blk_off).reshape(B, G, QB, M * NSA_BLOCK)
        mask = (jnp.repeat(ok, NSA_BLOCK, axis=-1) & (kpos <= pos[None, None, :, None]))[:, :, None]
        s_s = jnp.einsum('bqgrd,bgqkd->bgrqk', qb, kg, preferred_element_type=jnp.float32) * scale
        p_s = masked_softmax(s_s, mask)
        o_s = jnp.einsum('bgrqk,bgqkd->bqgrd', p_s.astype(vg.dtype), vg)
        lw = NSA_WINDOW + QB
        wkv = lax.dynamic_slice_in_dim(winp, P + i0 - w_start, lw, axis=1)
        kposw = P + i0 - NSA_WINDOW + jnp.arange(lw)
        maskw = ((kposw[None, :] <= pos[:, None]) & (kposw[None, :] > pos[:, None] - NSA_WINDOW)
                 & (kposw[None, :] >= w_start))
        s_w = jnp.einsum('bqgrd,bkgd->bgrqk', qb, wkv[:, :, 0], preferred_element_type=jnp.float32) * scale
        p_w = masked_softmax(s_w, maskw)
        o_w = jnp.einsum('bgrqk,bkgd->bqgrd', p_w.astype(wkv.dtype), wkv[:, :, 1])
        return jnp.stack([o_c, o_s, o_w], axis=-2)

    o = sweep(block, S, GATHER_Q_BLOCK, q)
    return jnp.sum(o * gates[..., None], axis=-2).reshape(B, S, G * R * dh)


def trunk_layer(x, P, past, li, norm_g, w_gu, w_dn, w_in, moba_qk_g, diff_qk_g, diff_lambda,
                diff_subln_g, nsa_qk_g, nsa_cmp_pos, nsa_cmp_w1, nsa_cmp_w2, w_branch, w_out):
    B, S, _ = x.shape
    pos = (P + jnp.arange(S)).astype(jnp.float32)
    p_moba, p_diff, p_sb, p_cmp, p_sel, p_win = past

    x = x + 0.5 * swiglu(rmsnorm(x, norm_g[0]), w_gu[0], w_dn[0])

    xn = rmsnorm(x, norm_g[1])
    (a_q, a_k, a_v, b_q, b_k, b_v, c_q, c_k, c_v, d_q, d_ck, d_cv, d_sk, d_sv, d_wk, d_wv,
     d_g, m_g) = jnp.split(xn @ w_in, IN_OFFSETS, axis=-1)
    hd = lambda t, *s: t.reshape((B, S) + s)
    qk = lambda t, g, *s: rope(rmsnorm(hd(t, *s), g), pos)

    new_a = jnp.stack([qk(a_k, moba_qk_g[1], N_HEADS, HEAD_DIM), hd(a_v, N_HEADS, HEAD_DIM)], axis=2)
    o_a = moba_attend(qk(a_q, moba_qk_g[0], N_HEADS, HEAD_DIM), jnp.concatenate([p_moba, new_a], axis=1), P)

    b_kr = qk(b_k, diff_qk_g[1], DIFF_KV_HEADS, 2, DIFF_SUB).reshape(B, S, DIFF_KV_HEADS, HEAD_DIM)
    new_b = jnp.stack([b_kr, hd(b_v, DIFF_KV_HEADS, HEAD_DIM)], axis=2)
    lam_init = 0.8 - 0.6 * math.exp(-0.3 * li)
    lv = diff_lambda.astype(jnp.float32)
    lam = jnp.exp(jnp.sum(lv[0] * lv[1])) - jnp.exp(jnp.sum(lv[2] * lv[3])) + lam_init
    o_b = diff_attend(qk(b_q, diff_qk_g[0], DIFF_KV_HEADS, DIFF_REP, 2, DIFF_SUB),
                      jnp.concatenate([p_diff, new_b], axis=1), P, lam, lam_init, diff_subln_g)

    new_c = jnp.stack([hd(c_k, N_HEADS, HEAD_DIM), hd(c_v, N_HEADS, HEAD_DIM)], axis=2)
    o_c = sb_attend(hd(c_q, N_HEADS, HEAD_DIM), jnp.concatenate([p_sb, new_c], axis=1), P)

    new_cmp = jnp.stack([qk(d_ck, nsa_qk_g[1], NSA_GROUPS, HEAD_DIM), hd(d_cv, NSA_GROUPS, HEAD_DIM)], axis=2)
    new_sel = jnp.stack([qk(d_sk, nsa_qk_g[2], NSA_GROUPS, HEAD_DIM), hd(d_sv, NSA_GROUPS, HEAD_DIM)], axis=2)
    new_win = jnp.stack([qk(d_wk, nsa_qk_g[3], NSA_GROUPS, HEAD_DIM), hd(d_wv, NSA_GROUPS, HEAD_DIM)], axis=2)
    win_kv = jnp.concatenate([p_win, new_win], axis=1)
    o_d = nsa_attend(qk(d_q, nsa_qk_g[0], NSA_GROUPS, NSA_REP, HEAD_DIM),
                     jnp.concatenate([p_cmp, new_cmp], axis=1), jnp.concatenate([p_sel, new_sel], axis=1),
                     win_kv, jax.nn.sigmoid(hd(d_g, NSA_GROUPS, NSA_REP, 3)), P, P - p_win.shape[1],
                     nsa_cmp_pos, nsa_cmp_w1, nsa_cmp_w2)
    win_state = win_kv[:, win_kv.shape[1] - min(NSA_WINDOW, P + S):]

    o = jnp.stack([o_a, o_b, o_c, o_d], axis=2)
    branches = jnp.einsum('bsmc,mcd->bsmd', o, w_branch)
    gates = jax.nn.sigmoid(hd(m_g, N_MIXERS, D_MODEL))
    x = x + jnp.sum(gates * branches, axis=2) @ w_out

    x = x + 0.5 * swiglu(rmsnorm(x, norm_g[2]), w_gu[1], w_dn[1])
    return x, (new_a, new_b, new_c, new_cmp, new_sel, win_state)


def setup_inputs(seed: int = 0) -> dict:
    key = jax.random.key(seed)
    keys = iter(jax.random.split(key, 32))
    f32 = jnp.float32

    def nrm(shape, scale=1.0):
        return jax.random.normal(next(keys), shape, f32) * scale

    def gain(shape):
        return 1.0 + nrm(shape, 0.02)

    n_pages = PAST_LEN // PAGE_SIZE
    n_used = DEC_BATCH * n_pages
    n_pool = n_used + max(1, n_used // 4)
    page_table = jax.random.permutation(next(keys), n_pool)[:n_used].reshape(DEC_BATCH, n_pages).astype(jnp.int32)
    win_rows = min(NSA_WINDOW, PAST_LEN)
    return {
        'x_prompt': nrm((BATCH, SEQ, D_MODEL)),
        'x_sample': nrm((DEC_BATCH, DEC_SEQ, D_MODEL)),
        'cache_moba_kv': nrm((DEPTH, n_pool, PAGE_SIZE, 2, N_HEADS, HEAD_DIM)),
        'cache_diff_kv': nrm((DEPTH, n_pool, PAGE_SIZE, 2, DIFF_KV_HEADS, HEAD_DIM)),
        'cache_sb_kv': nrm((DEPTH, n_pool, PAGE_SIZE, 2, N_HEADS, HEAD_DIM)),
        'cache_nsa_cmp_kv': nrm((DEPTH, n_pool, PAGE_SIZE, 2, NSA_GROUPS, HEAD_DIM)),
        'cache_nsa_sel_kv': nrm((DEPTH, n_pool, PAGE_SIZE, 2, NSA_GROUPS, HEAD_DIM)),
        'state_nsa_win_kv': nrm((DEPTH, DEC_BATCH, win_rows, 2, NSA_GROUPS, HEAD_DIM)),
        'page_table': page_table,
        'norm_g': gain((DEPTH, 3, D_MODEL)),
        'w_ffn_gate_up': nrm((DEPTH, 2, D_MODEL, 2 * D_FF), D_MODEL ** -0.5),
        'w_ffn_down': nrm((DEPTH, 2, D_FF, D_MODEL), D_FF ** -0.5),
        'w_in': nrm((DEPTH, D_MODEL, N_IN), D_MODEL ** -0.5),
        'moba_qk_g': gain((DEPTH, 2, HEAD_DIM)),
        'diff_qk_g': gain((DEPTH, 2, DIFF_SUB)),
        'diff_lambda': nrm((DEPTH, 4, DIFF_SUB), 0.1),
        'diff_subln_g': gain((DEPTH, HEAD_DIM)),
        'nsa_qk_g': gain((DEPTH, 4, HEAD_DIM)),
        'nsa_cmp_pos': nrm((DEPTH, 2, NSA_BLOCK, HEAD_DIM), 0.1),
        'nsa_cmp_w1': nrm((DEPTH, 2, NSA_BLOCK * HEAD_DIM, NSA_CMP_HIDDEN), (NSA_BLOCK * HEAD_DIM) ** -0.5),
        'nsa_cmp_w2': nrm((DEPTH, 2, NSA_CMP_HIDDEN, HEAD_DIM), NSA_CMP_HIDDEN ** -0.5),
        'w_branch': nrm((DEPTH, N_MIXERS, MIX_WIDTH, D_MODEL), MIX_WIDTH ** -0.5),
        'w_out': nrm((DEPTH, D_MODEL, D_MODEL), D_MODEL ** -0.5),
    }


def reference(x_prompt, x_sample, cache_moba_kv, cache_diff_kv, cache_sb_kv, cache_nsa_cmp_kv,
              cache_nsa_sel_kv, state_nsa_win_kv, page_table, norm_g, w_ffn_gate_up, w_ffn_down, w_in,
              moba_qk_g, diff_qk_g, diff_lambda, diff_subln_g, nsa_qk_g, nsa_cmp_pos, nsa_cmp_w1,
              nsa_cmp_w2, w_branch, w_out):
    past_len = page_table.shape[1] * PAGE_SIZE
    bp = x_prompt.shape[0]

    def empty(heads):
        return jnp.zeros((bp, 0, 2, heads, HEAD_DIM), x_prompt.dtype)

    past_prompt = (empty(N_HEADS), empty(DIFF_KV_HEADS), empty(N_HEADS),
                   empty(NSA_GROUPS), empty(NSA_GROUPS), empty(NSA_GROUPS))
    yp, ys = x_prompt, x_sample
    rows_prompt = [[] for _ in range(6)]
    rows_sample = [[] for _ in range(6)]
    for l in range(DEPTH):
        weights = (norm_g[l], w_ffn_gate_up[l], w_ffn_down[l], w_in[l], moba_qk_g[l], diff_qk_g[l],
                   diff_lambda[l], diff_subln_g[l], nsa_qk_g[l], nsa_cmp_pos[l], nsa_cmp_w1[l],
                   nsa_cmp_w2[l], w_branch[l], w_out[l])
        past_sample = (gather_pages(cache_moba_kv[l], page_table), gather_pages(cache_diff_kv[l], page_table),
                       gather_pages(cache_sb_kv[l], page_table), gather_pages(cache_nsa_cmp_kv[l], page_table),
                       gather_pages(cache_nsa_sel_kv[l], page_table), state_nsa_win_kv[l])
        yp, new_p = trunk_layer(yp, 0, past_prompt, l, *weights)
        ys, new_s = trunk_layer(ys, past_len, past_sample, l, *weights)
        for i in range(6):
            rows_prompt[i].append(new_p[i])
            rows_sample[i].append(new_s[i])
    (moba_kv_prompt, diff_kv_prompt, sb_kv_prompt, nsa_cmp_kv_prompt, nsa_sel_kv_prompt,
     nsa_win_kv_prompt) = [jnp.stack(r, axis=0) for r in rows_prompt]
    (moba_kv_sample, diff_kv_sample, sb_kv_sample, nsa_cmp_kv_sample, nsa_sel_kv_sample,
     nsa_win_kv_sample) = [jnp.stack(r, axis=0) for r in rows_sample]
    return (yp, ys, moba_kv_prompt, moba_kv_sample, diff_kv_prompt, diff_kv_sample, sb_kv_prompt, sb_kv_sample,
            nsa_cmp_kv_prompt, nsa_cmp_kv_sample, nsa_sel_kv_prompt, nsa_sel_kv_sample,
            nsa_win_kv_prompt, nsa_win_kv_sample)
```

```python
import functools
import math

import jax
import jax.numpy as jnp
import numpy as np
from jax import lax
from jax.experimental import pallas as pl
from jax.experimental.pallas import tpu as pltpu

F32 = jnp.float32
BF16 = jnp.bfloat16

HEAD_DIM = 128
N_MIXERS = 4
MOBA_BLOCK = 256
MOBA_TOPK = 3
NSA_BLOCK = 64
NSA_N_SEL = 16
NSA_WINDOW = 512
NSA_GROUPS = 2
ROPE_THETA = 10000.0
EPS = 1e-6
NEG_INF = -1e30

LANES = 128
BF16_SUBLANES = 16
MIB = 2 ** 20


def _cparams(sem, vmem_mib):
    return pltpu.CompilerParams(dimension_semantics=sem, vmem_limit_bytes=int(vmem_mib * MIB))


def _tile(n, pref):
    if n <= pref:
        return n
    t = pref - pref % 8
    while n % t:
        t -= 8
    return t


def _round_up(n, m):
    return -(-n // m) * m


def _log2(n):
    k = n.bit_length() - 1
    assert 1 << k == n, f"{n} must be a power of two"
    return k


def _row_positions(pos0, i, tq, rep):
    row = lax.broadcasted_iota(jnp.int32, (rep * tq, 1), 0)
    if rep > 1:
        row = row & (tq - 1) if tq & (tq - 1) == 0 else lax.rem(row, tq)
    return pos0 + i * tq + row


def _rmsnorm_kernel(x_ref, g_ref, o_ref):
    x = x_ref[...]
    ms = jnp.mean(x * x, axis=-1, keepdims=True)
    o_ref[...] = (x * lax.rsqrt(ms + EPS) * g_ref[...]).astype(o_ref.dtype)


def rmsnorm_cast(x, g):
    M, D = x.shape
    tm = _tile(M, 256)
    return pl.pallas_call(
        _rmsnorm_kernel,
        out_shape=jax.ShapeDtypeStruct((M, D), BF16),
        grid=(M // tm,),
        in_specs=[pl.BlockSpec((tm, D), lambda i: (i, 0)), pl.BlockSpec((1, D), lambda i: (0, 0))],
        out_specs=pl.BlockSpec((tm, D), lambda i: (i, 0)),
        compiler_params=_cparams(("parallel",), 32),
        name="rmsnorm_cast",
    )(x, g.reshape(1, D).astype(F32))


def _mm_kernel(x_ref, w_ref, o_ref):
    o_ref[...] = jnp.dot(x_ref[...], w_ref[...], preferred_element_type=F32).astype(o_ref.dtype)


def _mm_res_kernel(x_ref, w_ref, r_ref, o_ref, *, scale):
    acc = jnp.dot(x_ref[...], w_ref[...], preferred_element_type=F32)
    o_ref[...] = r_ref[...] + scale * acc


def matmul(x, w, *, tm, tn, resid=None, scale=1.0, name="mm"):
    M, K = x.shape
    N = w.shape[1]
    tm = _tile(M, tm)
    vmem = 2 * (tm * K * 2 + K * tn * 2 + tm * tn * 4 * (2 if resid is not None else 1)) / MIB + 6
    in_specs = [pl.BlockSpec((tm, K), lambda i, j: (i, 0)), pl.BlockSpec((K, tn), lambda i, j: (0, j))]
    args = [x, w]
    if resid is None:
        kern = _mm_kernel
    else:
        kern = functools.partial(_mm_res_kernel, scale=scale)
        in_specs.append(pl.BlockSpec((tm, tn), lambda i, j: (i, j)))
        args.append(resid)
    return pl.pallas_call(
        kern,
        out_shape=jax.ShapeDtypeStruct((M, N), F32),
        grid=(M // tm, N // tn),
        in_specs=in_specs,
        out_specs=pl.BlockSpec((tm, tn), lambda i, j: (i, j)),
        compiler_params=_cparams(("parallel", "arbitrary"), vmem),
        name=name,
    )(*args)


def _gateup_kernel(x_ref, wg_ref, wu_ref, o_ref):
    x = x_ref[...]
    g = jnp.dot(x, wg_ref[...], preferred_element_type=F32)
    u = jnp.dot(x, wu_ref[...], preferred_element_type=F32)
    o_ref[...] = (g / (1.0 + jnp.exp(-g)) * u).astype(o_ref.dtype)


def ffn_gateup(x, wgu, *, tm, tn):
    M, K = x.shape
    F = wgu.shape[1] // 2
    tm = _tile(M, tm)
    nf = F // tn
    vmem = 2 * (tm * K * 2 + 2 * K * tn * 2 + tm * tn * 2) / MIB + 8
    return pl.pallas_call(
        _gateup_kernel,
        out_shape=jax.ShapeDtypeStruct((M, F), BF16),
        grid=(M // tm, nf),
        in_specs=[pl.BlockSpec((tm, K), lambda i, j: (i, 0)),
                  pl.BlockSpec((K, tn), lambda i, j: (0, j)),
                  pl.BlockSpec((K, tn), lambda i, j: (0, j + nf))],
        out_specs=pl.BlockSpec((tm, tn), lambda i, j: (i, j)),
        compiler_params=_cparams(("parallel", "arbitrary"), vmem),
        name="ffn_gateup",
    )(x, wgu, wgu)


def _merge_kernel(xn_ref, o_ref, g0, g1, g2, g3, b0, b1, b2, b3, out_ref, *, mix_w):
    xn = xn_ref[...]
    acc = None
    for m, (g_ref, b_ref) in enumerate(((g0, b0), (g1, b1), (g2, b2), (g3, b3))):
        gate = jnp.dot(xn, g_ref[...], preferred_element_type=F32)
        gate = 1.0 / (1.0 + jnp.exp(-gate))
        br = jnp.dot(o_ref[:, m * mix_w:(m + 1) * mix_w], b_ref[...], preferred_element_type=F32)
        acc = gate * br if acc is None else acc + gate * br
    out_ref[...] = acc.astype(out_ref.dtype)


def merge(xn, o_all, w_mg, w_br, *, tm, tn):
    M, D = xn.shape
    mix_w = w_br.shape[1]
    tm = _tile(M, tm)
    nd = D // tn
    in_specs = [pl.BlockSpec((tm, D), lambda i, j: (i, 0)), pl.BlockSpec((tm, D), lambda i, j: (i, 0))]
    in_specs += [pl.BlockSpec((D, tn), functools.partial(lambda i, j, m: (0, m * nd + j), m=m)) for m in range(N_MIXERS)]
    in_specs += [pl.BlockSpec((None, mix_w, tn), functools.partial(lambda i, j, m: (m, 0, j), m=m)) for m in range(N_MIXERS)]
    vmem = 2 * (2 * tm * D * 2 + N_MIXERS * (D + mix_w) * tn * 2 + tm * tn * 2) / MIB + 8
    return pl.pallas_call(
        functools.partial(_merge_kernel, mix_w=mix_w),
        out_shape=jax.ShapeDtypeStruct((M, D), BF16),
        grid=(M // tm, nd),
        in_specs=in_specs,
        out_specs=pl.BlockSpec((tm, tn), lambda i, j: (i, j)),
        compiler_params=_cparams(("parallel", "arbitrary"), vmem),
        name="merge",
    )(xn, o_all, w_mg, w_mg, w_mg, w_mg, w_br, w_br, w_br, w_br)


def _col_offsets(d_model):
    mix = d_model // N_MIXERS
    h = mix // HEAD_DIM
    dkv = (h // 2) * HEAD_DIM
    nkv = NSA_GROUPS * HEAD_DIM
    sizes = [mix, mix, mix, mix, dkv, dkv, mix, mix, mix, mix, nkv, nkv, nkv, nkv, nkv, nkv, h * 3, N_MIXERS * d_model]
    offs = [0]
    for s in sizes:
        offs.append(offs[-1] + s)
    return offs


def _qkprep_kernel(p_ref, cs_ref, g_ref,
                   qa_ref, kva_ref, kvab_ref, kma_ref, qb_ref, kvb_ref, kvbb_ref, qc_ref, kvc_ref, kvcb_ref,
                   qd_ref, kcmp_ref, kcmpb_ref, ksel_ref, kselb_ref, kwin_ref, kwinb_ref, gd_ref,
                   *, offs, n_heads, tm):
    i = pl.program_id(1)
    hd = HEAD_DIM
    cos_f = cs_ref[:, 0:hd]
    sin_f = cs_ref[:, hd:2 * hd]
    cos_h = cs_ref[:, 2 * hd:3 * hd]
    sin_h = cs_ref[:, 3 * hd:4 * hd]
    lane = lax.broadcasted_iota(jnp.int32, (tm, hd), 1)
    lo_half = lane < hd // 2
    first_quarter = (lane & (hd // 2 - 1)) < hd // 4

    def nr_full(x, g):
        ms = jnp.mean(x * x, axis=-1, keepdims=True)
        y = x * lax.rsqrt(ms + EPS) * g
        return y * cos_f + pltpu.roll(y, hd // 2, 1) * sin_f

    def nr_half(x, g):
        x2 = x * x
        s_lo = jnp.sum(jnp.where(lo_half, x2, 0.0), axis=-1, keepdims=True)
        s_all = jnp.sum(x2, axis=-1, keepdims=True)
        ms = jnp.where(lo_half, s_lo, s_all - s_lo) * (2.0 / hd)
        y = x * lax.rsqrt(ms + EPS) * g
        rot = jnp.where(first_quarter, pltpu.roll(y, hd - hd // 4, 1), pltpu.roll(y, hd // 4, 1))
        return y * cos_h + rot * sin_h

    def col(c):
        return p_ref[:, c:c + hd]

    H = n_heads
    hkv_b = H // 2
    (o_aq, o_ak, o_av, o_bq, o_bk, o_bv, o_cq, o_ck, o_cv, o_dq,
     o_dck, o_dcv, o_dsk, o_dsv, o_dwk, o_dwv, o_dg, _o_mg, _end) = offs

    ksum = []
    for h in range(H):
        qa_ref[:, h * hd:(h + 1) * hd] = nr_full(col(o_aq + h * hd), g_ref[0:1, :]).astype(BF16)
        k = nr_full(col(o_ak + h * hd), g_ref[1:2, :])
        kva_ref[:, h * hd:(h + 1) * hd] = k
        kvab_ref[:, h * hd:(h + 1) * hd] = k.astype(BF16)
        ksum.append(jnp.sum(k, axis=0, keepdims=True))
        v = col(o_av + h * hd)
        kva_ref[:, (H + h) * hd:(H + h + 1) * hd] = v
        kvab_ref[:, (H + h) * hd:(H + h + 1) * hd] = v.astype(BF16)
    ksum = jnp.concatenate(ksum, axis=1) * (1.0 / MOBA_BLOCK)

    @pl.when(i == 0)
    def _():
        kma_ref[...] = jnp.zeros_like(kma_ref)

    blk_row = lax.broadcasted_iota(jnp.int32, kma_ref.shape, 0)
    kma_ref[...] += jnp.where(blk_row == (i * tm) // MOBA_BLOCK, ksum, 0.0)

    for h in range(H):
        q = nr_half(col(o_bq + h * hd), g_ref[2:3, :])
        qb_ref[:, (2 * h) * hd:(2 * h + 1) * hd] = jnp.where(lo_half, q, 0.0).astype(BF16)
        qb_ref[:, (2 * h + 1) * hd:(2 * h + 2) * hd] = jnp.where(lo_half, 0.0, q).astype(BF16)
    for h in range(hkv_b):
        k = nr_half(col(o_bk + h * hd), g_ref[3:4, :])
        kvb_ref[:, h * hd:(h + 1) * hd] = k
        kvbb_ref[:, h * hd:(h + 1) * hd] = k.astype(BF16)
        v = col(o_bv + h * hd)
        kvb_ref[:, (hkv_b + h) * hd:(hkv_b + h + 1) * hd] = v
        kvbb_ref[:, (hkv_b + h) * hd:(hkv_b + h + 1) * hd] = v.astype(BF16)

    for h in range(H):
        qc_ref[:, h * hd:(h + 1) * hd] = col(o_cq + h * hd).astype(BF16)
    for h in range(2 * H):
        kv = col(o_ck + h * hd)
        kvc_ref[:, h * hd:(h + 1) * hd] = kv
        kvcb_ref[:, h * hd:(h + 1) * hd] = kv.astype(BF16)

    for h in range(H):
        qd_ref[:, h * hd:(h + 1) * hd] = nr_full(col(o_dq + h * hd), g_ref[4:5, :]).astype(BF16)
    G = NSA_GROUPS
    for (ok, ov, grow, f_ref, b_ref) in ((o_dck, o_dcv, 5, kcmp_ref, kcmpb_ref),
                                         (o_dsk, o_dsv, 6, ksel_ref, kselb_ref),
                                         (o_dwk, o_dwv, 7, kwin_ref, kwinb_ref)):
        for g in range(G):
            k = nr_full(col(ok + g * hd), g_ref[grow:grow + 1, :])
            f_ref[:, g * hd:(g + 1) * hd] = k
            b_ref[:, g * hd:(g + 1) * hd] = k.astype(BF16)
            v = col(ov + g * hd)
            f_ref[:, (G + g) * hd:(G + g + 1) * hd] = v
            b_ref[:, (G + g) * hd:(G + g + 1) * hd] = v.astype(BF16)
    gd = col(o_dg)
    gd_ref[...] = 1.0 / (1.0 + jnp.exp(-gd))


def qkprep(proj, cs, gains, *, d_model):
    B, S, NP = proj.shape
    offs = tuple(_col_offsets(d_model))
    mix = d_model // N_MIXERS
    H = mix // HEAD_DIM
    dkv = (H // 2) * HEAD_DIM
    nkv = NSA_GROUPS * HEAD_DIM
    tm = _tile(S, 128)
    nbp = _round_up(-(-S // MOBA_BLOCK), LANES)

    def o(w, dt):
        return jax.ShapeDtypeStruct((B, S, w), dt), pl.BlockSpec((None, tm, w), lambda b, i: (b, i, 0))

    outs = [o(mix, BF16), o(2 * mix, F32), o(2 * mix, BF16),
            (jax.ShapeDtypeStruct((B, nbp, mix), F32), pl.BlockSpec((None, nbp, mix), lambda b, i: (b, 0, 0))),
            o(2 * mix, BF16), o(2 * dkv, F32), o(2 * dkv, BF16),
            o(mix, BF16), o(2 * mix, F32), o(2 * mix, BF16),
            o(mix, BF16), o(2 * nkv, F32), o(2 * nkv, BF16), o(2 * nkv, F32), o(2 * nkv, BF16),
            o(2 * nkv, F32), o(2 * nkv, BF16), o(LANES, F32)]
    res = pl.pallas_call(
        functools.partial(_qkprep_kernel, offs=offs, n_heads=H, tm=tm),
        out_shape=[s for s, _ in outs],
        grid=(B, S // tm),
        in_specs=[pl.BlockSpec((None, tm, NP), lambda b, i: (b, i, 0)),
                  pl.BlockSpec((tm, 4 * HEAD_DIM), lambda b, i: (i, 0)),
                  pl.BlockSpec((8, HEAD_DIM), lambda b, i: (0, 0))],
        out_specs=[s for _, s in outs],
        compiler_params=_cparams(("parallel", "arbitrary"), 48),
        name="qkprep",
    )(proj, cs, gains)
    keys = ("qa", "kva", "kvab", "kma", "qb", "kvb", "kvbb", "qc", "kvc", "kvcb",
            "qd", "kcmp", "kcmpb", "ksel", "kselb", "kwin", "kwinb", "gd")
    return dict(zip(keys, res))


def _gather_kernel(tbl_ref, pool_ref, tail_ref, o_ref, *rest, n_pages, km_cols, page):
    p = pl.program_id(1)
    if km_cols:
        km_ref = rest[0]

        @pl.when(p == 0)
        def _():
            km_ref[...] = jnp.zeros_like(km_ref)

    @pl.when(p < n_pages)
    def _():
        x = pool_ref[...]
        o_ref[...] = x.astype(o_ref.dtype)
        if km_cols:
            ksum = jnp.sum(x[:, :km_cols], axis=0, keepdims=True) * (1.0 / MOBA_BLOCK)
            blk_row = lax.broadcasted_iota(jnp.int32, km_ref.shape, 0)
            km_ref[...] += jnp.where(blk_row == (p * page) // MOBA_BLOCK, ksum, 0.0)

    @pl.when(p == n_pages)
    def _():
        row = lax.broadcasted_iota(jnp.int32, o_ref.shape, 0)
        o_ref[...] = jnp.where(row == 0, tail_ref[0:1, :], 0.0).astype(o_ref.dtype)

    @pl.when(p > n_pages)
    def _():
        o_ref[...] = jnp.zeros_like(o_ref)


def gather_cache(pool, page_table, tail, *, lp, out_dtype, with_kmean=False):
    n_pool, page, W = pool.shape
    B, n_pages = page_table.shape
    nblk = lp // page
    out_shape = [jax.ShapeDtypeStruct((B, lp, W), out_dtype)]
    out_specs = [pl.BlockSpec((None, page, W), lambda b, p, tbl: (b, p, 0))]
    km_cols = 0
    if with_kmean:
        km_cols = W // 2
        nbp = _round_up(-(-lp // MOBA_BLOCK), LANES)
        out_shape.append(jax.ShapeDtypeStruct((B, nbp, km_cols), F32))
        out_specs.append(pl.BlockSpec((None, nbp, km_cols), lambda b, p, tbl: (b, 0, 0)))
    grid_spec = pltpu.PrefetchScalarGridSpec(
        num_scalar_prefetch=1,
        grid=(B, nblk),
        in_specs=[pl.BlockSpec((None, page, W), lambda b, p, tbl: (tbl[b, jnp.minimum(p, n_pages - 1)], 0, 0)),
                  pl.BlockSpec((None, tail.shape[1], W), lambda b, p, tbl: (b, 0, 0))],
        out_specs=out_specs,
    )
    res = pl.pallas_call(
        functools.partial(_gather_kernel, n_pages=n_pages, km_cols=km_cols, page=page),
        out_shape=out_shape,
        grid_spec=grid_spec,
        compiler_params=_cparams(("parallel", "arbitrary"), 32),
        name="gather_cache",
    )(page_table, pool, tail)
    return res if with_kmean else res[0]


def _rank_select(score, cand, n_iota, n_real, n_keep):
    sm = jnp.where(cand, score, -jnp.inf)
    rank = jnp.zeros(score.shape, F32)
    for m in range(n_real):
        c = sm[:, m:m + 1]
        beats = (c > sm) | ((c == sm) & (m < n_iota))
        rank = rank + jnp.where(beats, 1.0, 0.0)
    return cand & (rank < n_keep)


def _moba_sel_kernel(q_ref, km_ref, o_ref, *, tq, n_heads, nb, nbp, pos0):
    i = pl.program_id(1)
    hd = HEAD_DIM
    pos = pos0 + i * tq + lax.broadcasted_iota(jnp.int32, (tq, 1), 0)
    own = pos >> _log2(MOBA_BLOCK)
    n_iota = lax.broadcasted_iota(jnp.int32, (tq, nbp), 1)
    cand = n_iota < own
    for h in range(n_heads):
        q = q_ref[:, h * hd:(h + 1) * hd]
        km = km_ref[:, h * hd:(h + 1) * hd].astype(BF16)
        gate = lax.dot_general(q, km, (((1,), (1,)), ((), ())), preferred_element_type=F32)
        sel = _rank_select(gate, cand, n_iota, nb, MOBA_TOPK) | (n_iota == own)
        o_ref[:, h * nbp:(h + 1) * nbp] = jnp.where(sel, 1.0, 0.0).astype(o_ref.dtype)


def moba_select(q, kmean, *, nb, pos0, tq):
    B, S, W = q.shape
    nbp = kmean.shape[1]
    H = W // HEAD_DIM
    return pl.pallas_call(
        functools.partial(_moba_sel_kernel, tq=tq, n_heads=H, nb=nb, nbp=nbp, pos0=pos0),
        out_shape=jax.ShapeDtypeStruct((B, S, H * nbp), BF16),
        grid=(B, S // tq),
        in_specs=[pl.BlockSpec((None, tq, W), lambda b, i: (b, i, 0)),
                  pl.BlockSpec((None, nbp, W), lambda b, i: (b, 0, 0))],
        out_specs=pl.BlockSpec((None, tq, H * nbp), lambda b, i: (b, i, 0)),
        compiler_params=_cparams(("parallel", "parallel"), 32),
        name="moba_select",
    )(q, kmean)


def _compress_kernel(x_ref, pos_ref, w1_ref, w2_ref, o_ref, acc_ref, *, nr):
    r = pl.program_id(1)
    hd = HEAD_DIM
    G = NSA_GROUPS

    @pl.when(r == 0)
    def _():
        acc_ref[...] = jnp.zeros_like(acc_ref)

    for kv in range(2):
        pe = pos_ref[kv]
        w = w1_ref[kv]
        for g in range(G):
            c = (kv * G + g) * hd
            xs = (x_ref[:, c:c + hd] + pe).astype(BF16)
            acc_ref[kv * G + g] += jnp.dot(xs, w, preferred_element_type=F32)

    @pl.when(r == nr - 1)
    def _():
        for kv in range(2):
            for g in range(G):
                a = acc_ref[kv * G + g]
                h = 0.5 * a * (1.0 + jnp.tanh(0.7978845608028654 * (a + 0.044715 * a * a * a)))
                c = (kv * G + g) * hd
                o_ref[:, c:c + hd] = jnp.dot(h.astype(BF16), w2_ref[kv], preferred_element_type=F32)


def nsa_compress(x, pos_emb, w1, w2):
    R = x.shape[0]
    hd = HEAD_DIM
    wrow = 2 * NSA_GROUPS * hd
    nr = x.shape[1] // wrow
    hid = w1.shape[2]
    tr = _tile(R, 1064)
    pos4 = pos_emb.reshape(2, nr, 1, hd).astype(F32)
    return pl.pallas_call(
        functools.partial(_compress_kernel, nr=nr),
        out_shape=jax.ShapeDtypeStruct((R, wrow), F32),
        grid=(R // tr, nr),
        in_specs=[pl.BlockSpec((tr, wrow), lambda t, r: (t, r)),
                  pl.BlockSpec((2, None, 1, hd), lambda t, r: (0, r, 0, 0)),
                  pl.BlockSpec((2, hd, hid), lambda t, r: (0, r, 0)),
                  pl.BlockSpec((2, hid, hd), lambda t, r: (0, 0, 0))],
        out_specs=pl.BlockSpec((tr, wrow), lambda t, r: (t, 0)),
        scratch_shapes=[pltpu.VMEM((2 * NSA_GROUPS, tr, hid), F32)],
        compiler_params=_cparams(("parallel", "arbitrary"), 32),
        name="nsa_compress",
    )(x, pos4, w1, w2)


def _nsa_cmp_kernel(q_ref, kc_ref, o_ref, sel_ref, *, tq, rep, nc, ncp, pos0, scale):
    i = pl.program_id(1)
    hd = HEAD_DIM
    G = NSA_GROUPS
    pos = _row_positions(pos0, i, tq, 1)
    own = pos >> _log2(NSA_BLOCK)
    n_iota = lax.broadcasted_iota(jnp.int32, (tq, ncp), 1)
    cand = (n_iota < own) & (n_iota < nc)
    pos_r = _row_positions(pos0, i, tq, rep)
    n_iota_r = lax.broadcasted_iota(jnp.int32, (rep * tq, ncp), 1)
    vis_r = ((n_iota_r + 1) * NSA_BLOCK <= pos_r + 1) & (n_iota_r < nc)
    for g in range(G):
        q = jnp.concatenate([q_ref[:, (g * rep + r) * hd:(g * rep + r + 1) * hd] for r in range(rep)], axis=0)
        kc = kc_ref[:, g * hd:(g + 1) * hd].astype(BF16)
        vc = kc_ref[:, (G + g) * hd:(G + g + 1) * hd].astype(BF16)
        s = lax.dot_general(q, kc, (((1,), (1,)), ((), ())), preferred_element_type=F32) * scale
        s = jnp.where(vis_r, s, NEG_INF)
        e = jnp.exp(s - jnp.max(s, axis=-1, keepdims=True))
        p = jnp.where(vis_r, e / jnp.sum(e, axis=-1, keepdims=True), 0.0)
        o = jnp.dot(p.astype(BF16), vc, preferred_element_type=F32)
        imp = p[0:tq]
        for r in range(rep):
            o_ref[:, (g * rep + r) * hd:(g * rep + r + 1) * hd] = o[r * tq:(r + 1) * tq].astype(o_ref.dtype)
            if r:
                imp = imp + p[r * tq:(r + 1) * tq]
        sel = _rank_select(imp, cand, n_iota, nc, NSA_N_SEL - 1) | (n_iota == own)
        sel_ref[:, g * ncp:(g + 1) * ncp] = jnp.where(sel, 1.0, 0.0).astype(sel_ref.dtype)


def nsa_cmp_attend(q, kc, *, nc, pos0, tq):
    B, S, W = q.shape
    ncp = kc.shape[1]
    H = W // HEAD_DIM
    rep = H // NSA_GROUPS
    return pl.pallas_call(
        functools.partial(_nsa_cmp_kernel, tq=tq, rep=rep, nc=nc, ncp=ncp, pos0=pos0, scale=HEAD_DIM ** -0.5),
        out_shape=[jax.ShapeDtypeStruct((B, S, W), BF16), jax.ShapeDtypeStruct((B, S, NSA_GROUPS * ncp), BF16)],
        grid=(B, S // tq),
        in_specs=[pl.BlockSpec((None, tq, W), lambda b, i: (b, i, 0)),
                  pl.BlockSpec((None, ncp, kc.shape[2]), lambda b, i: (b, 0, 0))],
        out_specs=[pl.BlockSpec((None, tq, W), lambda b, i: (b, i, 0)),
                   pl.BlockSpec((None, tq, NSA_GROUPS * ncp), lambda b, i: (b, i, 0))],
        compiler_params=_cparams(("parallel", "parallel"), 32),
        name="nsa_cmp_attend",
    )(q, kc)


def _fa_bounds(i, *, tq, tk, nkt, pos0, kpos0, window, minimum, maximum):
    last = pos0 + (i + 1) * tq - 1 - kpos0
    hi = minimum(last // tk, nkt - 1)
    if window is None:
        return 0, hi
    first = pos0 + i * tq - window + 1 - kpos0
    return maximum(first // tk, 0), hi


def _fa_kernel(*refs, tq, tk, hkv, rep, pos0, kpos0, n_keys, window, blk, nbp, scale, nkt, has_sel):
    if has_sel:
        q_ref, k_ref, v_ref, sel_ref, o_ref, m_sc, l_sc, acc_sc = refs
    else:
        q_ref, k_ref, v_ref, o_ref, m_sc, l_sc, acc_sc = refs
    i = pl.program_id(1)
    kb = pl.program_id(2)
    hd = HEAD_DIM
    lo, hi = _fa_bounds(i, tq=tq, tk=tk, nkt=nkt, pos0=pos0, kpos0=kpos0, window=window,
                        minimum=jnp.minimum, maximum=jnp.maximum)
    kt = kb + lo

    @pl.when(kb == 0)
    def _():
        m_sc[...] = jnp.full(m_sc.shape, NEG_INF, F32)
        l_sc[...] = jnp.zeros_like(l_sc)
        acc_sc[...] = jnp.zeros_like(acc_sc)

    @pl.when(kt <= hi)
    def _():
        qpos = _row_positions(pos0, i, tq, rep)
        kidx = kt * tk + lax.broadcasted_iota(jnp.int32, (1, tk), 1)
        kpos = kpos0 + kidx
        base = (kpos <= qpos) & (kidx < n_keys)
        if window is not None:
            base = base & (kpos > qpos - window)
        if has_sel:
            n_of_key = (kt * tk + lax.broadcasted_iota(jnp.int32, (nbp, tk), 1)) >> _log2(blk)
            expand = jnp.where(n_of_key == lax.broadcasted_iota(jnp.int32, (nbp, tk), 0), 1.0, 0.0).astype(BF16)
        for g in range(hkv):
            q = jnp.concatenate([q_ref[:, (g * rep + r) * hd:(g * rep + r + 1) * hd] for r in range(rep)], axis=0)
            k = k_ref[:, g * hd:(g + 1) * hd]
            v = v_ref[:, g * hd:(g + 1) * hd]
            s = lax.dot_general(q, k, (((1,), (1,)), ((), ())), preferred_element_type=F32) * scale
            ok = base
            if has_sel:
                picked = jnp.dot(sel_ref[:, g * nbp:(g + 1) * nbp], expand, preferred_element_type=F32)
                ok = ok & (jnp.concatenate([picked] * rep, axis=0) > 0.5)
            s = jnp.where(ok, s, NEG_INF)
            m_prev = m_sc[g]
            m_new = jnp.maximum(m_prev, jnp.max(s, axis=-1, keepdims=True))
            p = jnp.where(ok, jnp.exp(s - m_new), 0.0)
            alpha = jnp.exp(m_prev - m_new)
            l_sc[g] = alpha * l_sc[g] + jnp.sum(p, axis=-1, keepdims=True)
            acc_sc[g] = alpha * acc_sc[g] + jnp.dot(p.astype(BF16), v, preferred_element_type=F32)
            m_sc[g] = m_new

    @pl.when(kb == pl.num_programs(2) - 1)
    def _():
        for g in range(hkv):
            l = l_sc[g]
            o = acc_sc[g] / jnp.where(l > 0.0, l, 1.0)
            for r in range(rep):
                o_ref[:, (g * rep + r) * hd:(g * rep + r + 1) * hd] = o[r * tq:(r + 1) * tq].astype(o_ref.dtype)


def flash_attend(q, kv, sel=None, *, hkv, rep, pos0, kpos0, n_keys, window=None, blk=1, scale, out_dtype, tq, tk):
    B, S, Wq = q.shape
    Lp = kv.shape[1]
    hw = hkv * HEAD_DIM
    nq, nkt = S // tq, Lp // tk
    kw = dict(tq=tq, tk=tk, nkt=nkt, pos0=pos0, kpos0=kpos0, window=window)
    spans = [_fa_bounds(i, minimum=min, maximum=max, **kw) for i in range(nq)]
    nk = max(hi - lo + 1 for lo, hi in spans)

    def kt_of(i, kb):
        lo, hi = _fa_bounds(i, minimum=jnp.minimum, maximum=jnp.maximum, **kw)
        return jnp.minimum(kb + lo, hi)

    in_specs = [pl.BlockSpec((None, tq, Wq), lambda b, i, kb: (b, i, 0)),
                pl.BlockSpec((None, tk, hw), lambda b, i, kb: (b, kt_of(i, kb), 0)),
                pl.BlockSpec((None, tk, hw), lambda b, i, kb: (b, kt_of(i, kb), 1))]
    args = [q, kv, kv]
    nbp = 0
    if sel is not None:
        nbp = sel.shape[2] // hkv
        in_specs.append(pl.BlockSpec((None, tq, hkv * nbp), lambda b, i, kb: (b, i, 0)))
        args.append(sel)
    rows = rep * tq
    return pl.pallas_call(
        functools.partial(_fa_kernel, tq=tq, tk=tk, hkv=hkv, rep=rep, pos0=pos0, kpos0=kpos0, n_keys=n_keys,
                          window=window, blk=blk, nbp=nbp, scale=scale, nkt=nkt, has_sel=sel is not None),
        out_shape=jax.ShapeDtypeStruct((B, S, Wq), out_dtype),
        grid=(B, nq, nk),
        in_specs=in_specs,
        out_specs=pl.BlockSpec((None, tq, Wq), lambda b, i, kb: (b, i, 0)),
        scratch_shapes=[pltpu.VMEM((hkv, rows, 1), F32), pltpu.VMEM((hkv, rows, 1), F32),
                        pltpu.VMEM((hkv, rows, HEAD_DIM), F32)],
        compiler_params=_cparams(("parallel", "parallel", "arbitrary"), 40),
        name="flash_attend",
    )(*args)


def _sb_kernel(q_ref, k_ref, v_ref, o_ref, c_sc, acc_sc, *, tq, tk, n_heads, pos0, n_keys, scale, nkt):
    i = pl.program_id(1)
    kb = pl.program_id(2)
    hd = HEAD_DIM
    hi = jnp.minimum((pos0 + (i + 1) * tq - 1) // tk, nkt - 1)
    kt = hi - kb

    @pl.when(kb == 0)
    def _():
        c_sc[...] = jnp.zeros_like(c_sc)
        acc_sc[...] = jnp.zeros_like(acc_sc)

    @pl.when(kt >= 0)
    def _():
        qpos = pos0 + i * tq + lax.broadcasted_iota(jnp.int32, (tq, 1), 0)
        kidx = kt * tk + lax.broadcasted_iota(jnp.int32, (1, tk), 1)
        past = (kidx < qpos) & (kidx < n_keys)
        suffix = jnp.where(lax.broadcasted_iota(jnp.int32, (tk, tk), 0) >= lax.broadcasted_iota(jnp.int32, (tk, tk), 1),
                           1.0, 0.0).astype(BF16)
        for h in range(n_heads):
            q = q_ref[:, h * hd:(h + 1) * hd]
            k = k_ref[:, h * hd:(h + 1) * hd]
            v = v_ref[:, h * hd:(h + 1) * hd]
            z = lax.dot_general(q, k, (((1,), (1,)), ((), ())), preferred_element_type=F32) * scale
            log_sig = jnp.minimum(z, 0.0) - jnp.log(1.0 + jnp.exp(-jnp.abs(z)))
            log_keep = jnp.where(past, log_sig - z, 0.0)
            lk_hi = log_keep.astype(BF16)
            lk_lo = (log_keep - lk_hi.astype(F32)).astype(BF16)
            incl = (jnp.dot(lk_hi, suffix, preferred_element_type=F32)
                    + jnp.dot(lk_lo, suffix, preferred_element_type=F32))
            c = c_sc[h]
            log_after = c + incl - log_keep
            a = jnp.where(past, jnp.exp(log_sig + log_after), 0.0)
            acc_sc[h] += jnp.dot(a.astype(BF16), v, preferred_element_type=F32)
            c_sc[h] = c + incl[:, 0:1]

    @pl.when(kb == pl.num_programs(2) - 1)
    def _():
        for h in range(n_heads):
            o_ref[:, h * hd:(h + 1) * hd] = acc_sc[h].astype(o_ref.dtype)


def sb_attend(q, kv, *, pos0, n_keys, tq, tk):
    B, S, W = q.shape
    Lp = kv.shape[1]
    H = W // HEAD_DIM
    nq, nkt = S // tq, Lp // tk

    def hi_of(i, minimum):
        return minimum((pos0 + (i + 1) * tq - 1) // tk, nkt - 1)

    nk = max(hi_of(i, min) for i in range(nq)) + 1

    def kt_of(i, kb):
        return jnp.maximum(hi_of(i, jnp.minimum) - kb, 0)

    return pl.pallas_call(
        functools.partial(_sb_kernel, tq=tq, tk=tk, n_heads=H, pos0=pos0, n_keys=n_keys,
                          scale=HEAD_DIM ** -0.5, nkt=nkt),
        out_shape=jax.ShapeDtypeStruct((B, S, W), BF16),
        grid=(B, nq, nk),
        in_specs=[pl.BlockSpec((None, tq, W), lambda b, i, kb: (b, i, 0)),
                  pl.BlockSpec((None, tk, W), lambda b, i, kb: (b, kt_of(i, kb), 0)),
                  pl.BlockSpec((None, tk, W), lambda b, i, kb: (b, kt_of(i, kb), 1))],
        out_specs=pl.BlockSpec((None, tq, W), lambda b, i, kb: (b, i, 0)),
        scratch_shapes=[pltpu.VMEM((H, tq, 1), F32), pltpu.VMEM((H, tq, HEAD_DIM), F32)],
        compiler_params=_cparams(("parallel", "parallel", "arbitrary"), 40),
        name="sb_attend",
    )(q, kv, kv)


def _post_kernel(oa_ref, ob_ref, oc_ref, ocmp_ref, osel_ref, owin_ref, gd_ref, lam_ref, sub_ref, li_ref, out_ref,
                 *, n_heads):
    hd = HEAD_DIM
    H = n_heads
    mix = H * hd
    lv = lam_ref[...]
    lam_init = li_ref[:, 0:1]
    lam = (jnp.exp(jnp.sum(lv[0:1] * lv[1:2], axis=-1, keepdims=True))
           - jnp.exp(jnp.sum(lv[2:3] * lv[3:4], axis=-1, keepdims=True)) + lam_init)
    out_ref[:, 0:mix] = oa_ref[...]
    for h in range(H):
        d = ob_ref[:, (2 * h) * hd:(2 * h + 1) * hd] - lam * ob_ref[:, (2 * h + 1) * hd:(2 * h + 2) * hd]
        ms = jnp.mean(d * d, axis=-1, keepdims=True)
        y = d * lax.rsqrt(ms + EPS) * sub_ref[...]
        out_ref[:, mix + h * hd:mix + (h + 1) * hd] = (y * (1.0 - lam_init)).astype(out_ref.dtype)
    out_ref[:, 2 * mix:3 * mix] = oc_ref[...]
    gd = gd_ref[...]
    for h in range(H):
        sl = slice(h * hd, (h + 1) * hd)
        o = (ocmp_ref[:, sl].astype(F32) * gd[:, 3 * h:3 * h + 1]
             + osel_ref[:, sl].astype(F32) * gd[:, 3 * h + 1:3 * h + 2]
             + owin_ref[:, sl].astype(F32) * gd[:, 3 * h + 2:3 * h + 3])
        out_ref[:, 3 * mix + h * hd:3 * mix + (h + 1) * hd] = o.astype(out_ref.dtype)


def mixer_post(oa, ob, oc, ocmp, osel, owin, gd, diff_lambda, subln_g, lam_init):
    M, mix = oa.shape
    H = mix // HEAD_DIM
    tm = _tile(M, 256)
    li = jnp.full((1, LANES), lam_init, F32)

    def row(w):
        return pl.BlockSpec((tm, w), lambda i: (i, 0))

    def whole(shape):
        return pl.BlockSpec(shape, lambda i: (0,) * len(shape))

    return pl.pallas_call(
        functools.partial(_post_kernel, n_heads=H),
        out_shape=jax.ShapeDtypeStruct((M, N_MIXERS * mix), BF16),
        grid=(M // tm,),
        in_specs=[row(mix), row(2 * mix), row(mix), row(mix), row(mix), row(mix), row(LANES),
                  whole(diff_lambda.shape), whole((1, HEAD_DIM)), whole((1, LANES))],
        out_specs=row(N_MIXERS * mix),
        compiler_params=_cparams(("parallel",), 32),
        name="mixer_post",
    )(oa, ob, oc, ocmp, osel, owin, gd, diff_lambda.astype(F32), subln_g.reshape(1, HEAD_DIM).astype(F32), li)


def _rope_tables(pos):
    def tab(width):
        half = width // 2
        inv = ROPE_THETA ** (-jnp.arange(half, dtype=F32) / half)
        ang = pos[:, None] * inv[None, :]
        c, s = jnp.cos(ang), jnp.sin(ang)
        reps = HEAD_DIM // width
        return jnp.tile(jnp.concatenate([c, c], axis=1), (1, reps)), jnp.tile(jnp.concatenate([-s, s], axis=1), (1, reps))
    c1, s1 = tab(HEAD_DIM)
    c2, s2 = tab(HEAD_DIM // 2)
    return jnp.concatenate([c1, s1, c2, s2], axis=1)


def _ffn(x, g, wgu, wdn):
    xn = rmsnorm_cast(x, g)
    h = ffn_gateup(xn, wgu, tm=1024, tn=256)
    return matmul(h, wdn, tm=512, tn=256, resid=x, scale=0.5, name="ffn_down")


def _mixers(pre, *, B, S, d_model, pos0, past, page_table, lw, li):
    mix = d_model // N_MIXERS
    H = mix // HEAD_DIM
    decode = past is not None
    scale = HEAD_DIM ** -0.5

    if not decode:
        sq = S
        q = {k: pre[k] for k in ("qa", "qb", "qc", "qd")}
        kva, kvb, kvc, ksel, kwin = pre["kvab"], pre["kvbb"], pre["kvcb"], pre["kselb"], pre["kwinb"]
        kmean = pre["kma"]
        cmp_src = pre["kcmp"]
        n_keys = S
        lp = S
        win_kpos0, win_keys = 0, S
        tq, tk = _tile(S, 256), _tile(S, 256)
    else:
        sq = BF16_SUBLANES
        nb_seq = B

        def as_rows(a, dt):
            a = a.reshape(nb_seq, 1, a.shape[-1]).astype(dt)
            return jnp.pad(a, ((0, 0), (0, sq - 1), (0, 0)))

        q = {k: as_rows(pre[k], BF16) for k in ("qa", "qb", "qc", "qd")}
        n_past = page_table.shape[1] * past["moba"].shape[1]
        n_keys = n_past + 1
        lp = _round_up(n_keys, MOBA_BLOCK)

        def pool(name):
            c = past[name]
            return c.reshape(c.shape[0], c.shape[1], -1)

        def tail(name):
            return as_rows(pre[name], F32)[:, :8]

        kva, kmean = gather_cache(pool("moba"), page_table, tail("kva"), lp=lp, out_dtype=BF16, with_kmean=True)
        kvb = gather_cache(pool("diff"), page_table, tail("kvb"), lp=lp, out_dtype=BF16)
        kvc = gather_cache(pool("sb"), page_table, tail("kvc"), lp=lp, out_dtype=BF16)
        cmp_src = gather_cache(pool("cmp"), page_table, tail("kcmp"), lp=lp, out_dtype=F32)
        ksel = gather_cache(pool("sel"), page_table, tail("ksel"), lp=lp, out_dtype=BF16)
        win_full = past["win_full"]
        win_keys = win_full.shape[1]
        win_kpos0 = pos0 + 1 - win_keys
        tk = 256
        kwin = jnp.pad(win_full, ((0, 0), (0, _round_up(win_keys, tk) - win_keys), (0, 0))).astype(BF16)
        tq = sq

    nb = -(-n_keys // MOBA_BLOCK)
    sel_a = moba_select(q["qa"], kmean, nb=nb, pos0=pos0, tq=tq)
    oa = flash_attend(q["qa"], kva, sel_a, hkv=H, rep=1, pos0=pos0, kpos0=0, n_keys=n_keys, blk=MOBA_BLOCK,
                      scale=scale, out_dtype=BF16, tq=tq, tk=tk)
    ob = flash_attend(q["qb"], kvb, hkv=H // 2, rep=4, pos0=pos0, kpos0=0, n_keys=n_keys,
                      scale=(HEAD_DIM // 2) ** -0.5, out_dtype=F32, tq=tq, tk=tk)
    oc = sb_attend(q["qc"], kvc, pos0=pos0, n_keys=n_keys, tq=tq, tk=tk)
    nblk = lp // NSA_BLOCK
    nc = -(-n_keys // NSA_BLOCK)
    kc = nsa_compress(cmp_src.reshape(B * nblk, -1), lw["cmp_pos"], lw["cmp_w1"], lw["cmp_w2"])
    ncp = _round_up(nblk, LANES)
    kc = jnp.pad(kc.reshape(B, nblk, -1), ((0, 0), (0, ncp - nblk), (0, 0)))
    ocmp, sel_d = nsa_cmp_attend(q["qd"], kc, nc=nc, pos0=pos0, tq=tq)
    rep_d = H // NSA_GROUPS
    osel = flash_attend(q["qd"], ksel, sel_d, hkv=NSA_GROUPS, rep=rep_d, pos0=pos0, kpos0=0, n_keys=n_keys,
                        blk=NSA_BLOCK, scale=scale, out_dtype=BF16, tq=tq, tk=tk)
    owin = flash_attend(q["qd"], kwin, hkv=NSA_GROUPS, rep=rep_d, pos0=pos0, kpos0=win_kpos0, n_keys=win_keys,
                        window=NSA_WINDOW, scale=scale, out_dtype=BF16, tq=tq, tk=tk)

    def rows(a):
        return a[:, 0, :] if decode else a.reshape(B * S, a.shape[-1])

    gd = pre["gd"].reshape(-1, LANES)
    return mixer_post(rows(oa), rows(ob), rows(oc), rows(ocmp), rows(osel), rows(owin), gd,
                      lw["diff_lambda"], lw["diff_subln_g"], 0.8 - 0.6 * math.exp(-0.3 * li))


def _layer(x, *, B, S, pos0, past, page_table, lw, li):
    M, D = x.shape
    decode = past is not None
    x = _ffn(x, lw["norm_g"][0], lw["wgu"][0], lw["wdn"][0])
    xn = rmsnorm_cast(x, lw["norm_g"][1])
    proj = matmul(xn, lw["w_proj"], tm=1024, tn=256, name="in_proj")
    if decode:
        pos = jnp.full((M,), pos0, F32)
        pre = qkprep(proj.reshape(1, M, -1), _rope_tables(pos), lw["gains"], d_model=D)
    else:
        pos = (pos0 + jnp.arange(S)).astype(F32)
        pre = qkprep(proj.reshape(B, S, -1), _rope_tables(pos), lw["gains"], d_model=D)
    new_rows = {k: pre[k].reshape(B, S, -1) for k in ("kva", "kvb", "kvc", "kcmp", "ksel", "kwin")}
    if decode:
        past = dict(past)
        past["win_full"] = jnp.concatenate([past["win"].reshape(B, past["win"].shape[1], -1), new_rows["kwin"]], axis=1)
        win_all = past["win_full"]
    else:
        win_all = new_rows["kwin"]
    n_state = min(NSA_WINDOW, pos0 + S)
    win_state = win_all[:, win_all.shape[1] - n_state:]
    o_all = _mixers(pre, B=B, S=S, d_model=D, pos0=pos0, past=past, page_table=page_table, lw=lw, li=li)
    merged = merge(xn, o_all, lw["w_mg"], lw["w_br"], tm=512, tn=256)
    x = matmul(merged, lw["w_out"], tm=1024, tn=512, resid=x, scale=1.0, name="out_proj")
    x = _ffn(x, lw["norm_g"][2], lw["wgu"][1], lw["wdn"][1])
    return x, (new_rows["kva"], new_rows["kvb"], new_rows["kvc"], new_rows["kcmp"], new_rows["ksel"], win_state)


def kernel(x_prompt, x_sample, cache_moba_kv, cache_diff_kv, cache_sb_kv, cache_nsa_cmp_kv, cache_nsa_sel_kv,
           state_nsa_win_kv, page_table, norm_g, w_ffn_gate_up, w_ffn_down, w_in, moba_qk_g, diff_qk_g,
           diff_lambda, diff_subln_g, nsa_qk_g, nsa_cmp_pos, nsa_cmp_w1, nsa_cmp_w2, w_branch, w_out):
    bp, sp, D = x_prompt.shape
    bs, ss, _ = x_sample.shape
    assert ss == 1, "the decode group is one new row per sequence"
    depth = norm_g.shape[0]
    page = cache_moba_kv.shape[2]
    past_len = page_table.shape[1] * page
    offs = _col_offsets(D)
    o_mg = offs[17]
    n_proj = _round_up(o_mg, 256)
    H = D // N_MIXERS // HEAD_DIM
    heads = {"kva": H, "kvb": H // 2, "kvc": H, "kcmp": NSA_GROUPS, "ksel": NSA_GROUPS, "kwin": NSA_GROUPS}

    yp = x_prompt.reshape(bp * sp, D)
    ys = x_sample.reshape(bs * ss, D)
    rows_p = [[] for _ in range(6)]
    rows_s = [[] for _ in range(6)]
    for l in range(depth):
        half_tile = jnp.tile(diff_qk_g[l].astype(F32), (1, 2))
        lw = {
            "norm_g": norm_g[l],
            "wgu": [w_ffn_gate_up[l, j].astype(BF16) for j in range(2)],
            "wdn": [w_ffn_down[l, j].astype(BF16) for j in range(2)],
            "w_proj": w_in[l, :, :n_proj].astype(BF16),
            "w_mg": w_in[l, :, o_mg:].astype(BF16),
            "w_br": w_branch[l].astype(BF16),
            "w_out": w_out[l].astype(BF16),
            "gains": jnp.concatenate([moba_qk_g[l].astype(F32), half_tile, nsa_qk_g[l].astype(F32)], axis=0),
            "diff_lambda": diff_lambda[l],
            "diff_subln_g": diff_subln_g[l],
            "cmp_pos": nsa_cmp_pos[l],
            "cmp_w1": nsa_cmp_w1[l].astype(BF16),
            "cmp_w2": nsa_cmp_w2[l].astype(BF16),
        }
        past_s = {"moba": cache_moba_kv[l], "diff": cache_diff_kv[l], "sb": cache_sb_kv[l],
                  "cmp": cache_nsa_cmp_kv[l], "sel": cache_nsa_sel_kv[l], "win": state_nsa_win_kv[l]}
        yp, new_p = _layer(yp, B=bp, S=sp, pos0=0, past=None, page_table=None, lw=lw, li=l)
        ys, new_s = _layer(ys, B=bs, S=ss, pos0=past_len, past=past_s, page_table=page_table, lw=lw, li=l)
        for i in range(6):
            rows_p[i].append(new_p[i])
            rows_s[i].append(new_s[i])

    def stack(rows, name):
        a = jnp.stack(rows, axis=0)
        return a.reshape(a.shape[:3] + (2, heads[name], HEAD_DIM))

    names = ("kva", "kvb", "kvc", "kcmp", "ksel", "kwin")
    outs_p = [stack(rows_p[i], n) for i, n in enumerate(names)]
    outs_s = [stack(rows_s[i], n) for i, n in enumerate(names)]
    result = [yp.reshape(bp, sp, D), ys.reshape(bs, ss, D)]
    for a, b in zip(outs_p, outs_s):
        result += [a, b]
    return tuple(result)
```

```python
import functools
import math

import jax
import jax.numpy as jnp
import numpy as np
from jax import lax
from jax.experimental import pallas as pl
from jax.experimental.pallas import tpu as pltpu

F32 = jnp.float32
BF16 = jnp.bfloat16

HEAD_DIM = 128
N_MIXERS = 4
MOBA_BLOCK = 256
MOBA_TOPK = 3
NSA_BLOCK = 64
NSA_N_SEL = 16
NSA_WINDOW = 512
NSA_GROUPS = 2
ROPE_THETA = 10000.0
EPS = 1e-6
NEG_INF = -1e30

LANES = 128
BF16_SUBLANES = 16
MIB = 2 ** 20


def _cparams(sem, vmem_mib):
    return pltpu.CompilerParams(dimension_semantics=sem, vmem_limit_bytes=int(vmem_mib * MIB))


def _tile(n, pref):
    if n <= pref:
        return n
    t = pref - pref % 8
    while n % t:
        t -= 8
    return t


def _round_up(n, m):
    return -(-n // m) * m


def _log2(n):
    k = n.bit_length() - 1
    assert 1 << k == n, f"{n} must be a power of two"
    return k


def _row_positions(pos0, i, tq, rep):
    row = lax.broadcasted_iota(jnp.int32, (rep * tq, 1), 0)
    if rep > 1:
        row = row & (tq - 1) if tq & (tq - 1) == 0 else lax.rem(row, tq)
    return pos0 + i * tq + row


def _rmsnorm_kernel(x_ref, g_ref, o_ref):
    x = x_ref[...]
    ms = jnp.mean(x * x, axis=-1, keepdims=True)
    o_ref[...] = (x * lax.rsqrt(ms + EPS) * g_ref[...]).astype(o_ref.dtype)


def rmsnorm_cast(x, g):
    M, D = x.shape
    tm = _tile(M, 256)
    return pl.pallas_call(
        _rmsnorm_kernel,
        out_shape=jax.ShapeDtypeStruct((M, D), BF16),
        grid=(M // tm,),
        in_specs=[pl.BlockSpec((tm, D), lambda i: (i, 0)), pl.BlockSpec((1, D), lambda i: (0, 0))],
        out_specs=pl.BlockSpec((tm, D), lambda i: (i, 0)),
        compiler_params=_cparams(("parallel",), 32),
        name="rmsnorm_cast",
    )(x, g.reshape(1, D).astype(F32))


def _lead(ix_ref, n):
    return tuple(ix_ref[t] for t in range(n))


def _mm_kernel(ix_ref, x_ref, w_ref, o_ref):
    o_ref[...] = jnp.dot(x_ref[...], w_ref[...], preferred_element_type=F32).astype(o_ref.dtype)


def _mm_res_kernel(ix_ref, x_ref, w_ref, r_ref, o_ref, *, scale):
    acc = jnp.dot(x_ref[...], w_ref[...], preferred_element_type=F32)
    o_ref[...] = r_ref[...] + scale * acc


def matmul(x, w, widx, *, n_out, tm, tn, resid=None, scale=1.0, name="mm"):
    M, K = x.shape
    tm = _tile(M, tm)
    N = n_out
    nlead = len(widx)
    vmem = 2 * (tm * K * 2 + K * tn * 2 + tm * tn * 4 * (2 if resid is not None else 1)) / MIB + 6
    in_specs = [pl.BlockSpec((tm, K), lambda i, j, ix: (i, 0)),
                pl.BlockSpec((None,) * nlead + (K, tn), lambda i, j, ix: _lead(ix, nlead) + (0, j))]
    args = [x, w]
    if resid is None:
        kern = _mm_kernel
    else:
        kern = functools.partial(_mm_res_kernel, scale=scale)
        in_specs.append(pl.BlockSpec((tm, tn), lambda i, j, ix: (i, j)))
        args.append(resid)
    return pl.pallas_call(
        kern,
        out_shape=jax.ShapeDtypeStruct((M, N), F32),
        grid_spec=pltpu.PrefetchScalarGridSpec(
            num_scalar_prefetch=1, grid=(M // tm, N // tn), in_specs=in_specs,
            out_specs=pl.BlockSpec((tm, tn), lambda i, j, ix: (i, j))),
        compiler_params=_cparams(("parallel", "arbitrary"), vmem),
        name=name,
    )(jnp.asarray(widx, jnp.int32), *args)


def _gateup_kernel(ix_ref, x_ref, wg_ref, wu_ref, o_ref):
    x = x_ref[...]
    g = jnp.dot(x, wg_ref[...], preferred_element_type=F32)
    u = jnp.dot(x, wu_ref[...], preferred_element_type=F32)
    o_ref[...] = (g / (1.0 + jnp.exp(-g)) * u).astype(o_ref.dtype)


def ffn_gateup(x, wgu, widx, *, tm, tn):
    M, K = x.shape
    F = wgu.shape[-1] // 2
    tm = _tile(M, tm)
    nf = F // tn
    nlead = len(widx)
    vmem = 2 * (tm * K * 2 + 2 * K * tn * 2 + tm * tn * 2) / MIB + 8
    return pl.pallas_call(
        _gateup_kernel,
        out_shape=jax.ShapeDtypeStruct((M, F), BF16),
        grid_spec=pltpu.PrefetchScalarGridSpec(
            num_scalar_prefetch=1, grid=(M // tm, nf),
            in_specs=[pl.BlockSpec((tm, K), lambda i, j, ix: (i, 0)),
                      pl.BlockSpec((None,) * nlead + (K, tn), lambda i, j, ix: _lead(ix, nlead) + (0, j)),
                      pl.BlockSpec((None,) * nlead + (K, tn), lambda i, j, ix: _lead(ix, nlead) + (0, j + nf))],
            out_specs=pl.BlockSpec((tm, tn), lambda i, j, ix: (i, j))),
        compiler_params=_cparams(("parallel", "arbitrary"), vmem),
        name="ffn_gateup",
    )(jnp.asarray(widx, jnp.int32), x, wgu, wgu)


def _merge_kernel(ix_ref, xn_ref, o_ref, g0, g1, g2, g3, b0, b1, b2, b3, out_ref, *, mix_w):
    xn = xn_ref[...]
    acc = None
    for m, (g_ref, b_ref) in enumerate(((g0, b0), (g1, b1), (g2, b2), (g3, b3))):
        gate = jnp.dot(xn, g_ref[...], preferred_element_type=F32)
        gate = 1.0 / (1.0 + jnp.exp(-gate))
        br = jnp.dot(o_ref[:, m * mix_w:(m + 1) * mix_w], b_ref[...], preferred_element_type=F32)
        acc = gate * br if acc is None else acc + gate * br
    out_ref[...] = acc.astype(out_ref.dtype)


def merge(xn, o_all, w_mg, w_br, l, *, tm, tn):
    M, D = xn.shape
    mix_w = w_br.shape[2]
    tm = _tile(M, tm)
    nd = D // tn
    in_specs = [pl.BlockSpec((tm, D), lambda i, j, ix: (i, 0)), pl.BlockSpec((tm, D), lambda i, j, ix: (i, 0))]
    in_specs += [pl.BlockSpec((None, D, tn), functools.partial(lambda i, j, ix, m: (ix[0], 0, m * nd + j), m=m))
                 for m in range(N_MIXERS)]
    in_specs += [pl.BlockSpec((None, None, mix_w, tn), functools.partial(lambda i, j, ix, m: (ix[0], m, 0, j), m=m))
                 for m in range(N_MIXERS)]
    vmem = 2 * (2 * tm * D * 2 + N_MIXERS * (D + mix_w) * tn * 2 + tm * tn * 2) / MIB + 8
    return pl.pallas_call(
        functools.partial(_merge_kernel, mix_w=mix_w),
        out_shape=jax.ShapeDtypeStruct((M, D), BF16),
        grid_spec=pltpu.PrefetchScalarGridSpec(
            num_scalar_prefetch=1, grid=(M // tm, nd), in_specs=in_specs,
            out_specs=pl.BlockSpec((tm, tn), lambda i, j, ix: (i, j))),
        compiler_params=_cparams(("parallel", "arbitrary"), vmem),
        name="merge",
    )(jnp.asarray([l], jnp.int32), xn, o_all, w_mg, w_mg, w_mg, w_mg, w_br, w_br, w_br, w_br)


def _col_offsets(d_model):
    mix = d_model // N_MIXERS
    h = mix // HEAD_DIM
    dkv = (h // 2) * HEAD_DIM
    nkv = NSA_GROUPS * HEAD_DIM
    sizes = [mix, mix, mix, mix, dkv, dkv, mix, mix, mix, mix, nkv, nkv, nkv, nkv, nkv, nkv, h * 3, N_MIXERS * d_model]
    offs = [0]
    for s in sizes:
        offs.append(offs[-1] + s)
    return offs


def _qkprep_kernel(p_ref, cs_ref, g_ref,
                   qa_ref, kva_ref, kvab_ref, kma_ref, qb_ref, kvb_ref, kvbb_ref, qc_ref, kvc_ref, kvcb_ref,
                   qd_ref, kcmp_ref, kcmpb_ref, ksel_ref, kselb_ref, kwin_ref, kwinb_ref, gd_ref,
                   *, offs, n_heads, tm):
    i = pl.program_id(1)
    hd = HEAD_DIM
    cos_f = cs_ref[:, 0:hd]
    sin_f = cs_ref[:, hd:2 * hd]
    cos_h = cs_ref[:, 2 * hd:3 * hd]
    sin_h = cs_ref[:, 3 * hd:4 * hd]
    lane = lax.broadcasted_iota(jnp.int32, (tm, hd), 1)
    lo_half = lane < hd // 2
    first_quarter = (lane & (hd // 2 - 1)) < hd // 4

    def nr_full(x, g):
        ms = jnp.mean(x * x, axis=-1, keepdims=True)
        y = x * lax.rsqrt(ms + EPS) * g
        return y * cos_f + pltpu.roll(y, hd // 2, 1) * sin_f

    def nr_half(x, g):
        x2 = x * x
        s_lo = jnp.sum(jnp.where(lo_half, x2, 0.0), axis=-1, keepdims=True)
        s_all = jnp.sum(x2, axis=-1, keepdims=True)
        ms = jnp.where(lo_half, s_lo, s_all - s_lo) * (2.0 / hd)
        y = x * lax.rsqrt(ms + EPS) * g
        rot = jnp.where(first_quarter, pltpu.roll(y, hd - hd // 4, 1), pltpu.roll(y, hd // 4, 1))
        return y * cos_h + rot * sin_h

    def col(c):
        return p_ref[:, c:c + hd]

    H = n_heads
    hkv_b = H // 2
    (o_aq, o_ak, o_av, o_bq, o_bk, o_bv, o_cq, o_ck, o_cv, o_dq,
     o_dck, o_dcv, o_dsk, o_dsv, o_dwk, o_dwv, o_dg, _o_mg, _end) = offs

    ksum = []
    for h in range(H):
        qa_ref[:, h * hd:(h + 1) * hd] = nr_full(col(o_aq + h * hd), g_ref[0:1, :]).astype(BF16)
        k = nr_full(col(o_ak + h * hd), g_ref[1:2, :])
        kva_ref[:, h * hd:(h + 1) * hd] = k
        kvab_ref[:, h * hd:(h + 1) * hd] = k.astype(BF16)
        ksum.append(jnp.sum(k, axis=0, keepdims=True))
        v = col(o_av + h * hd)
        kva_ref[:, (H + h) * hd:(H + h + 1) * hd] = v
        kvab_ref[:, (H + h) * hd:(H + h + 1) * hd] = v.astype(BF16)
    ksum = jnp.concatenate(ksum, axis=1) * (1.0 / MOBA_BLOCK)

    @pl.when(i == 0)
    def _():
        kma_ref[...] = jnp.zeros_like(kma_ref)

    blk_row = lax.broadcasted_iota(jnp.int32, kma_ref.shape, 0)
    kma_ref[...] += jnp.where(blk_row == (i * tm) // MOBA_BLOCK, ksum, 0.0)

    for h in range(H):
        q = nr_half(col(o_bq + h * hd), g_ref[2:3, :])
        qb_ref[:, (2 * h) * hd:(2 * h + 1) * hd] = jnp.where(lo_half, q, 0.0).astype(BF16)
        qb_ref[:, (2 * h + 1) * hd:(2 * h + 2) * hd] = jnp.where(lo_half, 0.0, q).astype(BF16)
    for h in range(hkv_b):
        k = nr_half(col(o_bk + h * hd), g_ref[3:4, :])
        kvb_ref[:, h * hd:(h + 1) * hd] = k
        kvbb_ref[:, h * hd:(h + 1) * hd] = k.astype(BF16)
        v = col(o_bv + h * hd)
        kvb_ref[:, (hkv_b + h) * hd:(hkv_b + h + 1) * hd] = v
        kvbb_ref[:, (hkv_b + h) * hd:(hkv_b + h + 1) * hd] = v.astype(BF16)

    for h in range(H):
        qc_ref[:, h * hd:(h + 1) * hd] = col(o_cq + h * hd).astype(BF16)
    for h in range(2 * H):
        kv = col(o_ck + h * hd)
        kvc_ref[:, h * hd:(h + 1) * hd] = kv
        kvcb_ref[:, h * hd:(h + 1) * hd] = kv.astype(BF16)

    for h in range(H):
        qd_ref[:, h * hd:(h + 1) * hd] = nr_full(col(o_dq + h * hd), g_ref[4:5, :]).astype(BF16)
    G = NSA_GROUPS
    for (ok, ov, grow, f_ref, b_ref) in ((o_dck, o_dcv, 5, kcmp_ref, kcmpb_ref),
                                         (o_dsk, o_dsv, 6, ksel_ref, kselb_ref),
                                         (o_dwk, o_dwv, 7, kwin_ref, kwinb_ref)):
        for g in range(G):
            k = nr_full(col(ok + g * hd), g_ref[grow:grow + 1, :])
            f_ref[:, g * hd:(g + 1) * hd] = k
            b_ref[:, g * hd:(g + 1) * hd] = k.astype(BF16)
            v = col(ov + g * hd)
            f_ref[:, (G + g) * hd:(G + g + 1) * hd] = v
            b_ref[:, (G + g) * hd:(G + g + 1) * hd] = v.astype(BF16)
    gd = col(o_dg)
    gd_ref[...] = 1.0 / (1.0 + jnp.exp(-gd))


def qkprep(proj, cs, gains, *, d_model):
    B, S, NP = proj.shape
    offs = tuple(_col_offsets(d_model))
    mix = d_model // N_MIXERS
    H = mix // HEAD_DIM
    dkv = (H // 2) * HEAD_DIM
    nkv = NSA_GROUPS * HEAD_DIM
    tm = _tile(S, 128)
    nbp = _round_up(-(-S // MOBA_BLOCK), LANES)

    def o(w, dt):
        return jax.ShapeDtypeStruct((B, S, w), dt), pl.BlockSpec((None, tm, w), lambda b, i: (b, i, 0))

    outs = [o(mix, BF16), o(2 * mix, F32), o(2 * mix, BF16),
            (jax.ShapeDtypeStruct((B, nbp, mix), F32), pl.BlockSpec((None, nbp, mix), lambda b, i: (b, 0, 0))),
            o(2 * mix, BF16), o(2 * dkv, F32), o(2 * dkv, BF16),
            o(mix, BF16), o(2 * mix, F32), o(2 * mix, BF16),
            o(mix, BF16), o(2 * nkv, F32), o(2 * nkv, BF16), o(2 * nkv, F32), o(2 * nkv, BF16),
            o(2 * nkv, F32), o(2 * nkv, BF16), o(LANES, F32)]
    res = pl.pallas_call(
        functools.partial(_qkprep_kernel, offs=offs, n_heads=H, tm=tm),
        out_shape=[s for s, _ in outs],
        grid=(B, S // tm),
        in_specs=[pl.BlockSpec((None, tm, NP), lambda b, i: (b, i, 0)),
                  pl.BlockSpec((tm, 4 * HEAD_DIM), lambda b, i: (i, 0)),
                  pl.BlockSpec((8, HEAD_DIM), lambda b, i: (0, 0))],
        out_specs=[s for _, s in outs],
        compiler_params=_cparams(("parallel", "arbitrary"), 48),
        name="qkprep",
    )(proj, cs, gains)
    keys = ("qa", "kva", "kvab", "kma", "qb", "kvb", "kvbb", "qc", "kvc", "kvcb",
            "qd", "kcmp", "kcmpb", "ksel", "kselb", "kwin", "kwinb", "gd")
    return dict(zip(keys, res))


GATHER_PAGES_PER_STEP = 4


def _gather_kernel(tbl_ref, layer_ref, *refs, n_pages, pps, n_slots, page, with_km):
    pool_refs = refs[:pps]
    tail_ref, o_ref = refs[pps], refs[pps + 1]
    s = pl.program_id(1)
    hd = HEAD_DIM
    if with_km:
        km_ref = refs[pps + 2]

        @pl.when(s == 0)
        def _():
            km_ref[...] = jnp.zeros_like(km_ref)

    for k in range(pps):
        pg = s * pps + k
        rows = pl.ds(k * page, page)

        @pl.when(pg < n_pages)
        def _(k=k, pg=pg, rows=rows):
            for c in range(n_slots):
                x = pool_refs[k][pl.ds(c, page, stride=n_slots), :]
                o_ref[rows, c * hd:(c + 1) * hd] = x.astype(o_ref.dtype)
                if with_km and c < n_slots // 2:
                    ksum = jnp.sum(x, axis=0, keepdims=True) * (1.0 / MOBA_BLOCK)
                    blk_row = lax.broadcasted_iota(jnp.int32, (km_ref.shape[0], hd), 0)
                    km_ref[:, c * hd:(c + 1) * hd] += jnp.where(blk_row == (pg * page) // MOBA_BLOCK, ksum, 0.0)

        @pl.when(pg == n_pages)
        def _(rows=rows):
            row = lax.broadcasted_iota(jnp.int32, (page, o_ref.shape[1]), 0)
            o_ref[rows, :] = jnp.where(row == 0, tail_ref[0:1, :], 0.0).astype(o_ref.dtype)

        @pl.when(pg > n_pages)
        def _(rows=rows):
            o_ref[rows, :] = jnp.zeros((page, o_ref.shape[1]), o_ref.dtype)


def gather_cache(cache, l, page_table, tail, *, lp, out_dtype, with_kmean=False):
    depth, n_pool, page = cache.shape[:3]
    n_slots = cache.shape[3] * cache.shape[4]
    hd = cache.shape[5]
    W = n_slots * hd
    pool = cache.reshape(depth, n_pool, page * n_slots, hd)
    B, n_pages = page_table.shape
    pps = GATHER_PAGES_PER_STEP
    assert lp % (pps * page) == 0
    out_shape = [jax.ShapeDtypeStruct((B, lp, W), out_dtype)]
    out_specs = [pl.BlockSpec((None, pps * page, W), lambda b, s, tbl, lix: (b, s, 0))]
    if with_kmean:
        nbp = _round_up(-(-lp // MOBA_BLOCK), LANES)
        out_shape.append(jax.ShapeDtypeStruct((B, nbp, W // 2), F32))
        out_specs.append(pl.BlockSpec((None, nbp, W // 2), lambda b, s, tbl, lix: (b, 0, 0)))

    def page_spec(k):
        return pl.BlockSpec((None, None, page * n_slots, hd),
                            lambda b, s, tbl, lix: (lix[0], tbl[b, jnp.minimum(s * pps + k, n_pages - 1)], 0, 0))

    grid_spec = pltpu.PrefetchScalarGridSpec(
        num_scalar_prefetch=2,
        grid=(B, lp // (pps * page)),
        in_specs=[page_spec(k) for k in range(pps)]
        + [pl.BlockSpec((None, tail.shape[1], W), lambda b, s, tbl, lix: (b, 0, 0))],
        out_specs=out_specs,
    )
    res = pl.pallas_call(
        functools.partial(_gather_kernel, n_pages=n_pages, pps=pps, n_slots=n_slots, page=page, with_km=with_kmean),
        out_shape=out_shape,
        grid_spec=grid_spec,
        compiler_params=_cparams(("parallel", "arbitrary"), 40),
        name="gather_cache",
    )(page_table, jnp.asarray([l], jnp.int32), *([pool] * pps), tail)
    return res if with_kmean else res[0]


def _rank_select(score, cand, n_iota, n_real, n_keep):
    sm = jnp.where(cand, score, -jnp.inf)
    rank = jnp.zeros(score.shape, F32)
    for m in range(n_real):
        c = sm[:, m:m + 1]
        beats = (c > sm) | ((c == sm) & (m < n_iota))
        rank = rank + jnp.where(beats, 1.0, 0.0)
    return cand & (rank < n_keep)


def _moba_sel_kernel(q_ref, km_ref, o_ref, *, tq, n_heads, nb, nbp, pos0):
    i = pl.program_id(1)
    hd = HEAD_DIM
    pos = pos0 + i * tq + lax.broadcasted_iota(jnp.int32, (tq, 1), 0)
    own = pos >> _log2(MOBA_BLOCK)
    n_iota = lax.broadcasted_iota(jnp.int32, (tq, nbp), 1)
    cand = n_iota < own
    for h in range(n_heads):
        q = q_ref[:, h * hd:(h + 1) * hd]
        km = km_ref[:, h * hd:(h + 1) * hd].astype(BF16)
        gate = lax.dot_general(q, km, (((1,), (1,)), ((), ())), preferred_element_type=F32)
        sel = _rank_select(gate, cand, n_iota, nb, MOBA_TOPK) | (n_iota == own)
        o_ref[:, h * nbp:(h + 1) * nbp] = jnp.where(sel, 1.0, 0.0).astype(o_ref.dtype)


def moba_select(q, kmean, *, nb, pos0, tq):
    B, S, W = q.shape
    nbp = kmean.shape[1]
    H = W // HEAD_DIM
    return pl.pallas_call(
        functools.partial(_moba_sel_kernel, tq=tq, n_heads=H, nb=nb, nbp=nbp, pos0=pos0),
        out_shape=jax.ShapeDtypeStruct((B, S, H * nbp), BF16),
        grid=(B, S // tq),
        in_specs=[pl.BlockSpec((None, tq, W), lambda b, i: (b, i, 0)),
                  pl.BlockSpec((None, nbp, W), lambda b, i: (b, 0, 0))],
        out_specs=pl.BlockSpec((None, tq, H * nbp), lambda b, i: (b, i, 0)),
        compiler_params=_cparams(("parallel", "parallel"), 32),
        name="moba_select",
    )(q, kmean)


def _compress_kernel(x_ref, pos_ref, w1_ref, w2_ref, o_ref, acc_ref, *, nr):
    r = pl.program_id(1)
    hd = HEAD_DIM
    G = NSA_GROUPS

    @pl.when(r == 0)
    def _():
        acc_ref[...] = jnp.zeros_like(acc_ref)

    for kv in range(2):
        pe = pos_ref[kv]
        w = w1_ref[kv]
        for g in range(G):
            c = (kv * G + g) * hd
            xs = (x_ref[:, c:c + hd] + pe).astype(BF16)
            acc_ref[kv * G + g] += jnp.dot(xs, w, preferred_element_type=F32)

    @pl.when(r == nr - 1)
    def _():
        for kv in range(2):
            for g in range(G):
                a = acc_ref[kv * G + g]
                h = 0.5 * a * (1.0 + jnp.tanh(0.7978845608028654 * (a + 0.044715 * a * a * a)))
                c = (kv * G + g) * hd
                o_ref[:, c:c + hd] = jnp.dot(h.astype(BF16), w2_ref[kv], preferred_element_type=F32)


def nsa_compress(x, pos_emb, w1, w2):
    R = x.shape[0]
    hd = HEAD_DIM
    wrow = 2 * NSA_GROUPS * hd
    nr = x.shape[1] // wrow
    hid = w1.shape[2]
    tr = _tile(R, 1280)
    pos4 = pos_emb.reshape(2, nr, 1, hd).astype(F32)
    return pl.pallas_call(
        functools.partial(_compress_kernel, nr=nr),
        out_shape=jax.ShapeDtypeStruct((R, wrow), F32),
        grid=(R // tr, nr),
        in_specs=[pl.BlockSpec((tr, wrow), lambda t, r: (t, r)),
                  pl.BlockSpec((2, None, 1, hd), lambda t, r: (0, r, 0, 0)),
                  pl.BlockSpec((2, hd, hid), lambda t, r: (0, r, 0)),
                  pl.BlockSpec((2, hid, hd), lambda t, r: (0, 0, 0))],
        out_specs=pl.BlockSpec((tr, wrow), lambda t, r: (t, 0)),
        scratch_shapes=[pltpu.VMEM((2 * NSA_GROUPS, tr, hid), F32)],
        compiler_params=_cparams(("parallel", "arbitrary"), 32),
        name="nsa_compress",
    )(x, pos4, w1, w2)


def _nsa_cmp_kernel(q_ref, kc_ref, o_ref, sel_ref, *, tq, rep, nc, ncp, pos0, scale):
    i = pl.program_id(1)
    hd = HEAD_DIM
    G = NSA_GROUPS
    pos = _row_positions(pos0, i, tq, 1)
    own = pos >> _log2(NSA_BLOCK)
    n_iota = lax.broadcasted_iota(jnp.int32, (tq, ncp), 1)
    cand = (n_iota < own) & (n_iota < nc)
    pos_r = _row_positions(pos0, i, tq, rep)
    n_iota_r = lax.broadcasted_iota(jnp.int32, (rep * tq, ncp), 1)
    vis_r = ((n_iota_r + 1) * NSA_BLOCK <= pos_r + 1) & (n_iota_r < nc)
    for g in range(G):
        q = jnp.concatenate([q_ref[:, (g * rep + r) * hd:(g * rep + r + 1) * hd] for r in range(rep)], axis=0)
        kc = kc_ref[:, g * hd:(g + 1) * hd].astype(BF16)
        vc = kc_ref[:, (G + g) * hd:(G + g + 1) * hd].astype(BF16)
        s = lax.dot_general(q, kc, (((1,), (1,)), ((), ())), preferred_element_type=F32) * scale
        s = jnp.where(vis_r, s, NEG_INF)
        e = jnp.exp(s - jnp.max(s, axis=-1, keepdims=True))
        p = jnp.where(vis_r, e / jnp.sum(e, axis=-1, keepdims=True), 0.0)
        o = jnp.dot(p.astype(BF16), vc, preferred_element_type=F32)
        imp = p[0:tq]
        for r in range(rep):
            o_ref[:, (g * rep + r) * hd:(g * rep + r + 1) * hd] = o[r * tq:(r + 1) * tq].astype(o_ref.dtype)
            if r:
                imp = imp + p[r * tq:(r + 1) * tq]
        sel = _rank_select(imp, cand, n_iota, nc, NSA_N_SEL - 1) | (n_iota == own)
        sel_ref[:, g * ncp:(g + 1) * ncp] = jnp.where(sel, 1.0, 0.0).astype(sel_ref.dtype)


def nsa_cmp_attend(q, kc, *, nc, pos0, tq):
    B, S, W = q.shape
    ncp = kc.shape[1]
    H = W // HEAD_DIM
    rep = H // NSA_GROUPS
    return pl.pallas_call(
        functools.partial(_nsa_cmp_kernel, tq=tq, rep=rep, nc=nc, ncp=ncp, pos0=pos0, scale=HEAD_DIM ** -0.5),
        out_shape=[jax.ShapeDtypeStruct((B, S, W), BF16), jax.ShapeDtypeStruct((B, S, NSA_GROUPS * ncp), BF16)],
        grid=(B, S // tq),
        in_specs=[pl.BlockSpec((None, tq, W), lambda b, i: (b, i, 0)),
                  pl.BlockSpec((None, ncp, kc.shape[2]), lambda b, i: (b, 0, 0))],
        out_specs=[pl.BlockSpec((None, tq, W), lambda b, i: (b, i, 0)),
                   pl.BlockSpec((None, tq, NSA_GROUPS * ncp), lambda b, i: (b, i, 0))],
        compiler_params=_cparams(("parallel", "parallel"), 32),
        name="nsa_cmp_attend",
    )(q, kc)


def _fa_bounds(i, *, tq, tk, nkt, pos0, kpos0, window, minimum, maximum):
    last = pos0 + (i + 1) * tq - 1 - kpos0
    hi = minimum(last // tk, nkt - 1)
    if window is None:
        return 0, hi
    first = pos0 + i * tq - window + 1 - kpos0
    return maximum(first // tk, 0), hi


def _fa_kernel(*refs, tq, tk, hkv, rep, pos0, kpos0, n_keys, window, blk, nbp, scale, nkt, has_sel,
               self_in_first_tile):
    if has_sel:
        q_ref, k_ref, v_ref, sel_ref, o_ref, m_sc, l_sc, acc_sc = refs
    else:
        q_ref, k_ref, v_ref, o_ref, m_sc, l_sc, acc_sc = refs
    i = pl.program_id(1)
    kb = pl.program_id(2)
    hd = HEAD_DIM
    lo, hi = _fa_bounds(i, tq=tq, tk=tk, nkt=nkt, pos0=pos0, kpos0=kpos0, window=window,
                        minimum=jnp.minimum, maximum=jnp.maximum)
    kt = hi - kb

    @pl.when(kb == 0)
    def _():
        m_sc[...] = jnp.full(m_sc.shape, NEG_INF, F32)
        l_sc[...] = jnp.zeros_like(l_sc)
        acc_sc[...] = jnp.zeros_like(acc_sc)

    @pl.when(kt >= lo)
    def _():
        qpos = _row_positions(pos0, i, tq, rep)
        kidx = kt * tk + lax.broadcasted_iota(jnp.int32, (1, tk), 1)
        kpos = kpos0 + kidx
        base = (kpos <= qpos) & (kidx < n_keys)
        if window is not None:
            base = base & (kpos > qpos - window)
        if has_sel:
            n_of_key = (kt * tk + lax.broadcasted_iota(jnp.int32, (nbp, tk), 1)) >> _log2(blk)
            expand = jnp.where(n_of_key == lax.broadcasted_iota(jnp.int32, (nbp, tk), 0), 1.0, 0.0).astype(BF16)
        for g in range(hkv):
            q = jnp.concatenate([q_ref[:, (g * rep + r) * hd:(g * rep + r + 1) * hd] for r in range(rep)], axis=0)
            k = k_ref[:, g * hd:(g + 1) * hd]
            v = v_ref[:, g * hd:(g + 1) * hd]
            s = lax.dot_general(q, k, (((1,), (1,)), ((), ())), preferred_element_type=F32) * scale
            ok = base
            if has_sel:
                picked = jnp.dot(sel_ref[:, g * nbp:(g + 1) * nbp], expand, preferred_element_type=F32)
                ok = ok & (jnp.concatenate([picked] * rep, axis=0) > 0.5)
            s = jnp.where(ok, s, NEG_INF)
            m_prev = m_sc[g]
            m_new = jnp.maximum(m_prev, jnp.max(s, axis=-1, keepdims=True))
            p = jnp.exp(s - m_new)
            if not self_in_first_tile:
                p = jnp.where(ok, p, 0.0)
            alpha = jnp.exp(m_prev - m_new)
            l_sc[g] = alpha * l_sc[g] + jnp.sum(p, axis=-1, keepdims=True)
            acc_sc[g] = alpha * acc_sc[g] + jnp.dot(p.astype(BF16), v, preferred_element_type=F32)
            m_sc[g] = m_new

    @pl.when(kb == pl.num_programs(2) - 1)
    def _():
        for g in range(hkv):
            l = l_sc[g]
            o = acc_sc[g] / jnp.where(l > 0.0, l, 1.0)
            for r in range(rep):
                o_ref[:, (g * rep + r) * hd:(g * rep + r + 1) * hd] = o[r * tq:(r + 1) * tq].astype(o_ref.dtype)


def flash_attend(q, kv, sel=None, *, hkv, rep, pos0, kpos0, n_keys, window=None, blk=1, scale, out_dtype, tq, tk):
    B, S, Wq = q.shape
    Lp = kv.shape[1]
    hw = hkv * HEAD_DIM
    nq, nkt = S // tq, Lp // tk
    kw = dict(tq=tq, tk=tk, nkt=nkt, pos0=pos0, kpos0=kpos0, window=window)
    spans = [_fa_bounds(i, minimum=min, maximum=max, **kw) for i in range(nq)]
    nk = max(hi - lo + 1 for lo, hi in spans)

    def kt_of(i, kb):
        lo, hi = _fa_bounds(i, minimum=jnp.minimum, maximum=jnp.maximum, **kw)
        return jnp.maximum(hi - kb, lo)

    self_in_first_tile = (pos0 - kpos0) % tk + tq <= tk

    in_specs = [pl.BlockSpec((None, tq, Wq), lambda b, i, kb: (b, i, 0)),
                pl.BlockSpec((None, tk, hw), lambda b, i, kb: (b, kt_of(i, kb), 0)),
                pl.BlockSpec((None, tk, hw), lambda b, i, kb: (b, kt_of(i, kb), 1))]
    args = [q, kv, kv]
    nbp = 0
    if sel is not None:
        nbp = sel.shape[2] // hkv
        in_specs.append(pl.BlockSpec((None, tq, hkv * nbp), lambda b, i, kb: (b, i, 0)))
        args.append(sel)
    rows = rep * tq
    return pl.pallas_call(
        functools.partial(_fa_kernel, tq=tq, tk=tk, hkv=hkv, rep=rep, pos0=pos0, kpos0=kpos0, n_keys=n_keys,
                          window=window, blk=blk, nbp=nbp, scale=scale, nkt=nkt, has_sel=sel is not None,
                          self_in_first_tile=self_in_first_tile),
        out_shape=jax.ShapeDtypeStruct((B, S, Wq), out_dtype),
        grid=(B, nq, nk),
        in_specs=in_specs,
        out_specs=pl.BlockSpec((None, tq, Wq), lambda b, i, kb: (b, i, 0)),
        scratch_shapes=[pltpu.VMEM((hkv, rows, 1), F32), pltpu.VMEM((hkv, rows, 1), F32),
                        pltpu.VMEM((hkv, rows, HEAD_DIM), F32)],
        compiler_params=_cparams(("parallel", "parallel", "arbitrary"), 40),
        name="flash_attend",
    )(*args)


def _sb_kernel(q_ref, k_ref, v_ref, o_ref, c_sc, acc_sc, *, tq, tk, sub, n_heads, pos0, n_keys, scale, nkt,
               stack_heads):
    i = pl.program_id(1)
    kb = pl.program_id(2)
    hd = HEAD_DIM
    hi = jnp.minimum((pos0 + (i + 1) * tq - 1) // tk, nkt - 1)
    kt = hi - kb

    @pl.when(kb == 0)
    def _():
        c_sc[...] = jnp.zeros_like(c_sc)
        acc_sc[...] = jnp.zeros_like(acc_sc)

    @pl.when(kt >= 0)
    def _():
        qpos = pos0 + i * tq + lax.broadcasted_iota(jnp.int32, (tq, 1), 0)
        suffix = jnp.where(lax.broadcasted_iota(jnp.int32, (sub, sub), 0) >= lax.broadcasted_iota(jnp.int32, (sub, sub), 1),
                           1.0, 0.0).astype(BF16)
        def log_sigmoid(z):
            return jnp.minimum(z, 0.0) - jnp.log(1.0 + jnp.exp(-jnp.abs(z)))

        def suffix_sum(x):
            x_hi = x.astype(BF16)
            x_lo = (x - x_hi.astype(F32)).astype(BF16)
            return (jnp.dot(x_hi, suffix, preferred_element_type=F32)
                    + jnp.dot(x_lo, suffix, preferred_element_type=F32))

        if stack_heads:
            qpos_all = _row_positions(pos0, i, tq, n_heads)
            c = jnp.concatenate([c_sc[h] for h in range(n_heads)], axis=0)
            accs = [acc_sc[h] for h in range(n_heads)]
            for j in reversed(range(tk // sub)):
                kidx = kt * tk + j * sub + lax.broadcasted_iota(jnp.int32, (1, sub), 1)
                past = (kidx < qpos_all) & (kidx < n_keys)
                z = jnp.concatenate(
                    [lax.dot_general(q_ref[:, h * hd:(h + 1) * hd], k_ref[j * sub:(j + 1) * sub, h * hd:(h + 1) * hd],
                                     (((1,), (1,)), ((), ())), preferred_element_type=F32) for h in range(n_heads)],
                    axis=0) * scale
                log_sig = log_sigmoid(z)
                log_keep = jnp.where(past, log_sig - z, 0.0)
                incl = suffix_sum(log_keep)
                a = jnp.where(past, jnp.exp(log_sig + c + incl - log_keep), 0.0).astype(BF16)
                for h in range(n_heads):
                    accs[h] = accs[h] + jnp.dot(a[h * tq:(h + 1) * tq], v_ref[j * sub:(j + 1) * sub, h * hd:(h + 1) * hd],
                                                preferred_element_type=F32)
                c = c + incl[:, 0:1]
            for h in range(n_heads):
                acc_sc[h] = accs[h]
                c_sc[h] = c[h * tq:(h + 1) * tq]
            return

        pasts = []
        for j in range(tk // sub):
            kidx = kt * tk + j * sub + lax.broadcasted_iota(jnp.int32, (1, sub), 1)
            pasts.append((kidx < qpos) & (kidx < n_keys))
        for h in range(n_heads):
            q = q_ref[:, h * hd:(h + 1) * hd]
            c = c_sc[h]
            acc = acc_sc[h]
            for j in reversed(range(tk // sub)):
                past = pasts[j]
                k = k_ref[j * sub:(j + 1) * sub, h * hd:(h + 1) * hd]
                v = v_ref[j * sub:(j + 1) * sub, h * hd:(h + 1) * hd]
                z = lax.dot_general(q, k, (((1,), (1,)), ((), ())), preferred_element_type=F32) * scale
                log_sig = log_sigmoid(z)
                log_keep = jnp.where(past, log_sig - z, 0.0)
                incl = suffix_sum(log_keep)
                log_after = c + incl - log_keep
                a = jnp.where(past, jnp.exp(log_sig + log_after), 0.0)
                acc = acc + jnp.dot(a.astype(BF16), v, preferred_element_type=F32)
                c = c + incl[:, 0:1]
            acc_sc[h] = acc
            c_sc[h] = c

    @pl.when(kb == pl.num_programs(2) - 1)
    def _():
        for h in range(n_heads):
            o_ref[:, h * hd:(h + 1) * hd] = acc_sc[h].astype(o_ref.dtype)


SB_SUB_TILE = 256


def sb_attend(q, kv, *, pos0, n_keys, tq, tk):
    B, S, W = q.shape
    Lp = kv.shape[1]
    H = W // HEAD_DIM
    nq, nkt = S // tq, Lp // tk

    def hi_of(i, minimum):
        return minimum((pos0 + (i + 1) * tq - 1) // tk, nkt - 1)

    nk = max(hi_of(i, min) for i in range(nq)) + 1

    def kt_of(i, kb):
        return jnp.maximum(hi_of(i, jnp.minimum) - kb, 0)

    return pl.pallas_call(
        functools.partial(_sb_kernel, tq=tq, tk=tk, sub=min(tk, SB_SUB_TILE), n_heads=H, pos0=pos0, n_keys=n_keys,
                          scale=HEAD_DIM ** -0.5, nkt=nkt, stack_heads=H * tq <= SB_SUB_TILE),
        out_shape=jax.ShapeDtypeStruct((B, S, W), BF16),
        grid=(B, nq, nk),
        in_specs=[pl.BlockSpec((None, tq, W), lambda b, i, kb: (b, i, 0)),
                  pl.BlockSpec((None, tk, W), lambda b, i, kb: (b, kt_of(i, kb), 0)),
                  pl.BlockSpec((None, tk, W), lambda b, i, kb: (b, kt_of(i, kb), 1))],
        out_specs=pl.BlockSpec((None, tq, W), lambda b, i, kb: (b, i, 0)),
        scratch_shapes=[pltpu.VMEM((H, tq, 1), F32), pltpu.VMEM((H, tq, HEAD_DIM), F32)],
        compiler_params=_cparams(("parallel", "parallel", "arbitrary"), 40),
        name="sb_attend",
    )(q, kv, kv)


def _post_kernel(oa_ref, ob_ref, oc_ref, ocmp_ref, osel_ref, owin_ref, gd_ref, lam_ref, sub_ref, li_ref, out_ref,
                 *, n_heads):
    hd = HEAD_DIM
    H = n_heads
    mix = H * hd
    lv = lam_ref[...]
    lam_init = li_ref[:, 0:1]
    lam = (jnp.exp(jnp.sum(lv[0:1] * lv[1:2], axis=-1, keepdims=True))
           - jnp.exp(jnp.sum(lv[2:3] * lv[3:4], axis=-1, keepdims=True)) + lam_init)
    out_ref[:, 0:mix] = oa_ref[...]
    for h in range(H):
        d = ob_ref[:, (2 * h) * hd:(2 * h + 1) * hd] - lam * ob_ref[:, (2 * h + 1) * hd:(2 * h + 2) * hd]
        ms = jnp.mean(d * d, axis=-1, keepdims=True)
        y = d * lax.rsqrt(ms + EPS) * sub_ref[...]
        out_ref[:, mix + h * hd:mix + (h + 1) * hd] = (y * (1.0 - lam_init)).astype(out_ref.dtype)
    out_ref[:, 2 * mix:3 * mix] = oc_ref[...]
    gd = gd_ref[...]
    for h in range(H):
        sl = slice(h * hd, (h + 1) * hd)
        o = (ocmp_ref[:, sl].astype(F32) * gd[:, 3 * h:3 * h + 1]
             + osel_ref[:, sl].astype(F32) * gd[:, 3 * h + 1:3 * h + 2]
             + owin_ref[:, sl].astype(F32) * gd[:, 3 * h + 2:3 * h + 3])
        out_ref[:, 3 * mix + h * hd:3 * mix + (h + 1) * hd] = o.astype(out_ref.dtype)


def mixer_post(oa, ob, oc, ocmp, osel, owin, gd, diff_lambda, subln_g, lam_init):
    M, mix = oa.shape
    H = mix // HEAD_DIM
    tm = _tile(M, 256)
    li = jnp.full((1, LANES), lam_init, F32)

    def row(w):
        return pl.BlockSpec((tm, w), lambda i: (i, 0))

    def whole(shape):
        return pl.BlockSpec(shape, lambda i: (0,) * len(shape))

    return pl.pallas_call(
        functools.partial(_post_kernel, n_heads=H),
        out_shape=jax.ShapeDtypeStruct((M, N_MIXERS * mix), BF16),
        grid=(M // tm,),
        in_specs=[row(mix), row(2 * mix), row(mix), row(mix), row(mix), row(mix), row(LANES),
                  whole(diff_lambda.shape), whole((1, HEAD_DIM)), whole((1, LANES))],
        out_specs=row(N_MIXERS * mix),
        compiler_params=_cparams(("parallel",), 32),
        name="mixer_post",
    )(oa, ob, oc, ocmp, osel, owin, gd, diff_lambda.astype(F32), subln_g.reshape(1, HEAD_DIM).astype(F32), li)


def _rope_tables(pos):
    def tab(width):
        half = width // 2
        inv = ROPE_THETA ** (-jnp.arange(half, dtype=F32) / half)
        ang = pos[:, None] * inv[None, :]
        c, s = jnp.cos(ang), jnp.sin(ang)
        reps = HEAD_DIM // width
        return jnp.tile(jnp.concatenate([c, c], axis=1), (1, reps)), jnp.tile(jnp.concatenate([-s, s], axis=1), (1, reps))
    c1, s1 = tab(HEAD_DIM)
    c2, s2 = tab(HEAD_DIM // 2)
    return jnp.concatenate([c1, s1, c2, s2], axis=1)


DECODE_KEY_TILE = 1024


def _ffn(x, g, W, l, j):
    xn = rmsnorm_cast(x, g)
    h = ffn_gateup(xn, W["wgu"], (l, j), tm=1024, tn=256)
    return matmul(h, W["wdn"], (l, j), n_out=x.shape[1], tm=512, tn=256, resid=x, scale=0.5, name="ffn_down")


def _mixers(pre, *, B, S, d_model, pos0, past, page_table, lw, li):
    mix = d_model // N_MIXERS
    H = mix // HEAD_DIM
    decode = past is not None
    scale = HEAD_DIM ** -0.5

    if not decode:
        sq = S
        q = {k: pre[k] for k in ("qa", "qb", "qc", "qd")}
        kva, kvb, kvc, ksel, kwin = pre["kvab"], pre["kvbb"], pre["kvcb"], pre["kselb"], pre["kwinb"]
        kmean = pre["kma"]
        cmp_src = pre["kcmp"]
        n_keys = S
        lp = S
        win_kpos0, win_keys = 0, S
        tq, tk = _tile(S, 256), _tile(S, 256)
        tk_win = tk
    else:
        sq = BF16_SUBLANES
        nb_seq = B

        def as_rows(a, dt):
            a = a.reshape(nb_seq, 1, a.shape[-1]).astype(dt)
            return jnp.pad(a, ((0, 0), (0, sq - 1), (0, 0)))

        q = {k: as_rows(pre[k], BF16) for k in ("qa", "qb", "qc", "qd")}
        n_past = page_table.shape[1] * past["moba"].shape[2]
        n_keys = n_past + 1
        tk = DECODE_KEY_TILE
        lp = _round_up(n_keys, tk)

        def tail(name):
            return as_rows(pre[name], F32)[:, :8]

        def gather(cache, name, dt, **kw):
            return gather_cache(past[cache], li, page_table, tail(name), lp=lp, out_dtype=dt, **kw)

        kva, kmean = gather("moba", "kva", BF16, with_kmean=True)
        kvb = gather("diff", "kvb", BF16)
        kvc = gather("sb", "kvc", BF16)
        cmp_src = gather("cmp", "kcmp", F32)
        ksel = gather("sel", "ksel", BF16)
        win_full = past["win_full"]
        win_keys = win_full.shape[1]
        win_kpos0 = pos0 + 1 - win_keys
        tk_win = 256
        kwin = jnp.pad(win_full, ((0, 0), (0, _round_up(win_keys, tk_win) - win_keys), (0, 0))).astype(BF16)
        tq = sq

    nb = -(-n_keys // MOBA_BLOCK)
    sel_a = moba_select(q["qa"], kmean, nb=nb, pos0=pos0, tq=tq)
    oa = flash_attend(q["qa"], kva, sel_a, hkv=H, rep=1, pos0=pos0, kpos0=0, n_keys=n_keys, blk=MOBA_BLOCK,
                      scale=scale, out_dtype=BF16, tq=tq, tk=tk)
    ob = flash_attend(q["qb"], kvb, hkv=H // 2, rep=4, pos0=pos0, kpos0=0, n_keys=n_keys,
                      scale=(HEAD_DIM // 2) ** -0.5, out_dtype=F32, tq=tq, tk=tk)
    oc = sb_attend(q["qc"], kvc, pos0=pos0, n_keys=n_keys, tq=tq, tk=tk)
    nblk = lp // NSA_BLOCK
    nc = -(-n_keys // NSA_BLOCK)
    kc = nsa_compress(cmp_src.reshape(B * nblk, -1), lw["cmp_pos"], lw["cmp_w1"], lw["cmp_w2"])
    ncp = _round_up(nblk, LANES)
    kc = jnp.pad(kc.reshape(B, nblk, -1), ((0, 0), (0, ncp - nblk), (0, 0)))
    ocmp, sel_d = nsa_cmp_attend(q["qd"], kc, nc=nc, pos0=pos0, tq=tq)
    rep_d = H // NSA_GROUPS
    osel = flash_attend(q["qd"], ksel, sel_d, hkv=NSA_GROUPS, rep=rep_d, pos0=pos0, kpos0=0, n_keys=n_keys,
                        blk=NSA_BLOCK, scale=scale, out_dtype=BF16, tq=tq, tk=tk)
    owin = flash_attend(q["qd"], kwin, hkv=NSA_GROUPS, rep=rep_d, pos0=pos0, kpos0=win_kpos0, n_keys=win_keys,
                        window=NSA_WINDOW, scale=scale, out_dtype=BF16, tq=tq, tk=tk_win)

    def rows(a):
        return a[:, 0, :] if decode else a.reshape(B * S, a.shape[-1])

    gd = pre["gd"].reshape(-1, LANES)
    return mixer_post(rows(oa), rows(ob), rows(oc), rows(ocmp), rows(osel), rows(owin), gd,
                      lw["diff_lambda"], lw["diff_subln_g"], 0.8 - 0.6 * math.exp(-0.3 * li))


def _layer(x, *, B, S, pos0, past, page_table, W, lw, li):
    M, D = x.shape
    decode = past is not None
    x = _ffn(x, lw["norm_g"][0], W, li, 0)
    xn = rmsnorm_cast(x, lw["norm_g"][1])
    proj = matmul(xn, W["w_proj"], (li,), n_out=W["w_proj"].shape[2], tm=1024, tn=256, name="in_proj")
    if decode:
        pos = jnp.full((M,), pos0, F32)
        pre = qkprep(proj.reshape(1, M, -1), _rope_tables(pos), lw["gains"], d_model=D)
    else:
        pos = (pos0 + jnp.arange(S)).astype(F32)
        pre = qkprep(proj.reshape(B, S, -1), _rope_tables(pos), lw["gains"], d_model=D)
    new_rows = {k: pre[k].reshape(B, S, -1) for k in ("kva", "kvb", "kvc", "kcmp", "ksel", "kwin")}
    if decode:
        past = dict(past)
        past["win_full"] = jnp.concatenate([past["win"].reshape(B, past["win"].shape[1], -1), new_rows["kwin"]], axis=1)
        win_all = past["win_full"]
    else:
        win_all = new_rows["kwin"]
    n_state = min(NSA_WINDOW, pos0 + S)
    win_state = win_all[:, win_all.shape[1] - n_state:]
    o_all = _mixers(pre, B=B, S=S, d_model=D, pos0=pos0, past=past, page_table=page_table, lw=lw, li=li)
    merged = merge(xn, o_all, W["w_mg"], W["w_br"], li, tm=512, tn=256)
    x = matmul(merged, W["w_out"], (li,), n_out=D, tm=1024, tn=512, resid=x, scale=1.0, name="out_proj")
    x = _ffn(x, lw["norm_g"][2], W, li, 1)
    return x, (new_rows["kva"], new_rows["kvb"], new_rows["kvc"], new_rows["kcmp"], new_rows["ksel"], win_state)


def kernel(x_prompt, x_sample, cache_moba_kv, cache_diff_kv, cache_sb_kv, cache_nsa_cmp_kv, cache_nsa_sel_kv,
           state_nsa_win_kv, page_table, norm_g, w_ffn_gate_up, w_ffn_down, w_in, moba_qk_g, diff_qk_g,
           diff_lambda, diff_subln_g, nsa_qk_g, nsa_cmp_pos, nsa_cmp_w1, nsa_cmp_w2, w_branch, w_out):
    bp, sp, D = x_prompt.shape
    bs, ss, _ = x_sample.shape
    assert ss == 1, "the decode group is one new row per sequence"
    depth = norm_g.shape[0]
    page = cache_moba_kv.shape[2]
    past_len = page_table.shape[1] * page
    offs = _col_offsets(D)
    o_mg = offs[17]
    n_proj = _round_up(o_mg, 256)
    H = D // N_MIXERS // HEAD_DIM
    heads = {"kva": H, "kvb": H // 2, "kvc": H, "kcmp": NSA_GROUPS, "ksel": NSA_GROUPS, "kwin": NSA_GROUPS}

    yp = x_prompt.reshape(bp * sp, D)
    ys = x_sample.reshape(bs * ss, D)
    rows_p = [[] for _ in range(6)]
    rows_s = [[] for _ in range(6)]
    W = {
        "wgu": w_ffn_gate_up.astype(BF16),
        "wdn": w_ffn_down.astype(BF16),
        "w_proj": w_in[:, :, :n_proj].astype(BF16),
        "w_mg": w_in[:, :, o_mg:].astype(BF16),
        "w_br": w_branch.astype(BF16),
        "w_out": w_out.astype(BF16),
    }
    caches = {"moba": cache_moba_kv, "diff": cache_diff_kv, "sb": cache_sb_kv,
              "cmp": cache_nsa_cmp_kv, "sel": cache_nsa_sel_kv}
    for l in range(depth):
        half_tile = jnp.tile(diff_qk_g[l].astype(F32), (1, 2))
        lw = {
            "norm_g": norm_g[l],
            "gains": jnp.concatenate([moba_qk_g[l].astype(F32), half_tile, nsa_qk_g[l].astype(F32)], axis=0),
            "diff_lambda": diff_lambda[l],
            "diff_subln_g": diff_subln_g[l],
            "cmp_pos": nsa_cmp_pos[l],
            "cmp_w1": nsa_cmp_w1[l].astype(BF16),
            "cmp_w2": nsa_cmp_w2[l].astype(BF16),
        }
        past_s = dict(caches, win=state_nsa_win_kv[l])
        yp, new_p = _layer(yp, B=bp, S=sp, pos0=0, past=None, page_table=None, W=W, lw=lw, li=l)
        ys, new_s = _layer(ys, B=bs, S=ss, pos0=past_len, past=past_s, page_table=page_table, W=W, lw=lw, li=l)
        for i in range(6):
            rows_p[i].append(new_p[i])
            rows_s[i].append(new_s[i])

    def stack(rows, name):
        a = jnp.stack(rows, axis=0)
        return a.reshape(a.shape[:3] + (2, heads[name], HEAD_DIM))

    names = ("kva", "kvb", "kvc", "kcmp", "ksel", "kwin")
    outs_p = [stack(rows_p[i], n) for i, n in enumerate(names)]
    outs_s = [stack(rows_s[i], n) for i, n in enumerate(names)]
    result = [yp.reshape(bp, sp, D), ys.reshape(bs, ss, D)]
    for a, b in zip(outs_p, outs_s):
        result += [a, b]
    return tuple(result)
```

```python
import functools
import math

import jax
import jax.numpy as jnp
import numpy as np
from jax import lax
from jax.experimental import pallas as pl
from jax.experimental.pallas import tpu as pltpu

F32 = jnp.float32
BF16 = jnp.bfloat16

HEAD_DIM = 128
N_MIXERS = 4
MOBA_BLOCK = 256
MOBA_TOPK = 3
NSA_BLOCK = 64
NSA_N_SEL = 16
NSA_WINDOW = 512
NSA_GROUPS = 2
ROPE_THETA = 10000.0
EPS = 1e-6
NEG_INF = -1e30

LANES = 128
BF16_SUBLANES = 16
MIB = 2 ** 20


def _cparams(sem, vmem_mib):
    return pltpu.CompilerParams(dimension_semantics=sem, vmem_limit_bytes=int(vmem_mib * MIB))


def _tile(n, pref):
    if n <= pref:
        return n
    t = pref - pref % 8
    while n % t:
        t -= 8
    return t


def _round_up(n, m):
    return -(-n // m) * m


def _log2(n):
    k = n.bit_length() - 1
    assert 1 << k == n, f"{n} must be a power of two"
    return k


def _row_positions(pos0, i, tq, rep):
    row = lax.broadcasted_iota(jnp.int32, (rep * tq, 1), 0)
    if rep > 1:
        row = row & (tq - 1) if tq & (tq - 1) == 0 else lax.rem(row, tq)
    return pos0 + i * tq + row


def _rmsnorm_kernel(x_ref, g_ref, o_ref):
    x = x_ref[...]
    ms = jnp.mean(x * x, axis=-1, keepdims=True)
    o_ref[...] = (x * lax.rsqrt(ms + EPS) * g_ref[...]).astype(o_ref.dtype)


def rmsnorm_cast(x, g):
    M, D = x.shape
    tm = _tile(M, 256)
    return pl.pallas_call(
        _rmsnorm_kernel,
        out_shape=jax.ShapeDtypeStruct((M, D), BF16),
        grid=(M // tm,),
        in_specs=[pl.BlockSpec((tm, D), lambda i: (i, 0)), pl.BlockSpec((1, D), lambda i: (0, 0))],
        out_specs=pl.BlockSpec((tm, D), lambda i: (i, 0)),
        compiler_params=_cparams(("parallel",), 32),
        name="rmsnorm_cast",
    )(x, g.reshape(1, D).astype(F32))


def _lead(ix_ref, n):
    return tuple(ix_ref[t] for t in range(n))


def _mm_kernel(ix_ref, x_ref, w_ref, o_ref):
    o_ref[...] = jnp.dot(x_ref[...], w_ref[...], preferred_element_type=F32).astype(o_ref.dtype)


def _mm_res_kernel(ix_ref, x_ref, w_ref, r_ref, o_ref, *, scale):
    acc = jnp.dot(x_ref[...], w_ref[...], preferred_element_type=F32)
    o_ref[...] = r_ref[...] + scale * acc


def matmul(x, w, widx, *, n_out, tm, tn, resid=None, scale=1.0, name="mm"):
    M, K = x.shape
    tm = _tile(M, tm)
    N = n_out
    nlead = len(widx)
    vmem = 2 * (tm * K * 2 + K * tn * 2 + tm * tn * 4 * (2 if resid is not None else 1)) / MIB + 6
    in_specs = [pl.BlockSpec((tm, K), lambda i, j, ix: (i, 0)),
                pl.BlockSpec((None,) * nlead + (K, tn), lambda i, j, ix: _lead(ix, nlead) + (0, j))]
    args = [x, w]
    if resid is None:
        kern = _mm_kernel
    else:
        kern = functools.partial(_mm_res_kernel, scale=scale)
        in_specs.append(pl.BlockSpec((tm, tn), lambda i, j, ix: (i, j)))
        args.append(resid)
    return pl.pallas_call(
        kern,
        out_shape=jax.ShapeDtypeStruct((M, N), F32),
        grid_spec=pltpu.PrefetchScalarGridSpec(
            num_scalar_prefetch=1, grid=(M // tm, N // tn), in_specs=in_specs,
            out_specs=pl.BlockSpec((tm, tn), lambda i, j, ix: (i, j))),
        compiler_params=_cparams(("parallel", "arbitrary"), vmem),
        name=name,
    )(jnp.asarray(widx, jnp.int32), *args)


def _gateup_kernel(ix_ref, x_ref, wg_ref, wu_ref, o_ref):
    x = x_ref[...]
    g = jnp.dot(x, wg_ref[...].astype(BF16), preferred_element_type=F32)
    u = jnp.dot(x, wu_ref[...].astype(BF16), preferred_element_type=F32)
    o_ref[...] = (g / (1.0 + jnp.exp(-g)) * u).astype(o_ref.dtype)


def ffn_gateup(x, wgu, widx, *, tm, tn):
    M, K = x.shape
    F = wgu.shape[-1] // 2
    tm = _tile(M, tm)
    nf = F // tn
    nlead = len(widx)
    wbytes = wgu.dtype.itemsize
    vmem = (2 * (tm * K * 2 + 2 * K * tn * wbytes + tm * tn * 2) + 2 * K * tn * 2) / MIB + 8
    return pl.pallas_call(
        _gateup_kernel,
        out_shape=jax.ShapeDtypeStruct((M, F), BF16),
        grid_spec=pltpu.PrefetchScalarGridSpec(
            num_scalar_prefetch=1, grid=(M // tm, nf),
            in_specs=[pl.BlockSpec((tm, K), lambda i, j, ix: (i, 0)),
                      pl.BlockSpec((None,) * nlead + (K, tn), lambda i, j, ix: _lead(ix, nlead) + (0, j)),
                      pl.BlockSpec((None,) * nlead + (K, tn), lambda i, j, ix: _lead(ix, nlead) + (0, j + nf))],
            out_specs=pl.BlockSpec((tm, tn), lambda i, j, ix: (i, j))),
        compiler_params=_cparams(("parallel", "arbitrary"), vmem),
        name="ffn_gateup",
    )(jnp.asarray(widx, jnp.int32), x, wgu, wgu)


def _merge_kernel(ix_ref, xn_ref, o_ref, g0, g1, g2, g3, b0, b1, b2, b3, out_ref, *, mix_w):
    xn = xn_ref[...]
    acc = None
    for m, (g_ref, b_ref) in enumerate(((g0, b0), (g1, b1), (g2, b2), (g3, b3))):
        gate = jnp.dot(xn, g_ref[...], preferred_element_type=F32)
        gate = 1.0 / (1.0 + jnp.exp(-gate))
        br = jnp.dot(o_ref[:, m * mix_w:(m + 1) * mix_w], b_ref[...], preferred_element_type=F32)
        acc = gate * br if acc is None else acc + gate * br
    out_ref[...] = acc.astype(out_ref.dtype)


def merge(xn, o_all, w_mg, w_br, l, *, tm, tn):
    M, D = xn.shape
    mix_w = w_br.shape[2]
    tm = _tile(M, tm)
    nd = D // tn
    in_specs = [pl.BlockSpec((tm, D), lambda i, j, ix: (i, 0)), pl.BlockSpec((tm, D), lambda i, j, ix: (i, 0))]
    in_specs += [pl.BlockSpec((None, D, tn), functools.partial(lambda i, j, ix, m: (ix[0], 0, m * nd + j), m=m))
                 for m in range(N_MIXERS)]
    in_specs += [pl.BlockSpec((None, None, mix_w, tn), functools.partial(lambda i, j, ix, m: (ix[0], m, 0, j), m=m))
                 for m in range(N_MIXERS)]
    vmem = 2 * (2 * tm * D * 2 + N_MIXERS * (D + mix_w) * tn * 2 + tm * tn * 2) / MIB + 8
    return pl.pallas_call(
        functools.partial(_merge_kernel, mix_w=mix_w),
        out_shape=jax.ShapeDtypeStruct((M, D), BF16),
        grid_spec=pltpu.PrefetchScalarGridSpec(
            num_scalar_prefetch=1, grid=(M // tm, nd), in_specs=in_specs,
            out_specs=pl.BlockSpec((tm, tn), lambda i, j, ix: (i, j))),
        compiler_params=_cparams(("parallel", "arbitrary"), vmem),
        name="merge",
    )(jnp.asarray([l], jnp.int32), xn, o_all, w_mg, w_mg, w_mg, w_mg, w_br, w_br, w_br, w_br)


def _col_offsets(d_model):
    mix = d_model // N_MIXERS
    h = mix // HEAD_DIM
    dkv = (h // 2) * HEAD_DIM
    nkv = NSA_GROUPS * HEAD_DIM
    sizes = [mix, mix, mix, mix, dkv, dkv, mix, mix, mix, mix, nkv, nkv, nkv, nkv, nkv, nkv, h * 3, N_MIXERS * d_model]
    offs = [0]
    for s in sizes:
        offs.append(offs[-1] + s)
    return offs


def _qkprep_kernel(p_ref, cs_ref, g_ref,
                   qa_ref, kva_ref, kvab_ref, kma_ref, qb_ref, kvb_ref, kvbb_ref, qc_ref, kvc_ref, kvcb_ref,
                   qd_ref, kcmp_ref, kcmpb_ref, ksel_ref, kselb_ref, kwin_ref, kwinb_ref, gd_ref,
                   *, offs, n_heads, tm):
    i = pl.program_id(1)
    hd = HEAD_DIM
    cos_f = cs_ref[:, 0:hd]
    sin_f = cs_ref[:, hd:2 * hd]
    cos_h = cs_ref[:, 2 * hd:3 * hd]
    sin_h = cs_ref[:, 3 * hd:4 * hd]
    lane = lax.broadcasted_iota(jnp.int32, (tm, hd), 1)
    lo_half = lane < hd // 2
    first_quarter = (lane & (hd // 2 - 1)) < hd // 4

    def nr_full(x, g):
        ms = jnp.mean(x * x, axis=-1, keepdims=True)
        y = x * lax.rsqrt(ms + EPS) * g
        return y * cos_f + pltpu.roll(y, hd // 2, 1) * sin_f

    def nr_half(x, g):
        x2 = x * x
        s_lo = jnp.sum(jnp.where(lo_half, x2, 0.0), axis=-1, keepdims=True)
        s_all = jnp.sum(x2, axis=-1, keepdims=True)
        ms = jnp.where(lo_half, s_lo, s_all - s_lo) * (2.0 / hd)
        y = x * lax.rsqrt(ms + EPS) * g
        rot = jnp.where(first_quarter, pltpu.roll(y, hd - hd // 4, 1), pltpu.roll(y, hd // 4, 1))
        return y * cos_h + rot * sin_h

    def col(c):
        return p_ref[:, c:c + hd]

    H = n_heads
    hkv_b = H // 2
    (o_aq, o_ak, o_av, o_bq, o_bk, o_bv, o_cq, o_ck, o_cv, o_dq,
     o_dck, o_dcv, o_dsk, o_dsv, o_dwk, o_dwv, o_dg, _o_mg, _end) = offs

    ksum = []
    for h in range(H):
        qa_ref[:, h * hd:(h + 1) * hd] = nr_full(col(o_aq + h * hd), g_ref[0:1, :]).astype(BF16)
        k = nr_full(col(o_ak + h * hd), g_ref[1:2, :])
        kva_ref[:, h * hd:(h + 1) * hd] = k
        kvab_ref[:, h * hd:(h + 1) * hd] = k.astype(BF16)
        ksum.append(jnp.sum(k, axis=0, keepdims=True))
        v = col(o_av + h * hd)
        kva_ref[:, (H + h) * hd:(H + h + 1) * hd] = v
        kvab_ref[:, (H + h) * hd:(H + h + 1) * hd] = v.astype(BF16)
    ksum = jnp.concatenate(ksum, axis=1) * (1.0 / MOBA_BLOCK)

    @pl.when(i == 0)
    def _():
        kma_ref[...] = jnp.zeros_like(kma_ref)

    blk_row = lax.broadcasted_iota(jnp.int32, kma_ref.shape, 0)
    kma_ref[...] += jnp.where(blk_row == (i * tm) // MOBA_BLOCK, ksum, 0.0)

    for h in range(H):
        q = nr_half(col(o_bq + h * hd), g_ref[2:3, :])
        qb_ref[:, (2 * h) * hd:(2 * h + 1) * hd] = jnp.where(lo_half, q, 0.0).astype(BF16)
        qb_ref[:, (2 * h + 1) * hd:(2 * h + 2) * hd] = jnp.where(lo_half, 0.0, q).astype(BF16)
    for h in range(hkv_b):
        k = nr_half(col(o_bk + h * hd), g_ref[3:4, :])
        kvb_ref[:, h * hd:(h + 1) * hd] = k
        kvbb_ref[:, h * hd:(h + 1) * hd] = k.astype(BF16)
        v = col(o_bv + h * hd)
        kvb_ref[:, (hkv_b + h) * hd:(hkv_b + h + 1) * hd] = v
        kvbb_ref[:, (hkv_b + h) * hd:(hkv_b + h + 1) * hd] = v.astype(BF16)

    for h in range(H):
        qc_ref[:, h * hd:(h + 1) * hd] = col(o_cq + h * hd).astype(BF16)
    for h in range(2 * H):
        kv = col(o_ck + h * hd)
        kvc_ref[:, h * hd:(h + 1) * hd] = kv
        kvcb_ref[:, h * hd:(h + 1) * hd] = kv.astype(BF16)

    for h in range(H):
        qd_ref[:, h * hd:(h + 1) * hd] = nr_full(col(o_dq + h * hd), g_ref[4:5, :]).astype(BF16)
    G = NSA_GROUPS
    for (ok, ov, grow, f_ref, b_ref) in ((o_dck, o_dcv, 5, kcmp_ref, kcmpb_ref),
                                         (o_dsk, o_dsv, 6, ksel_ref, kselb_ref),
                                         (o_dwk, o_dwv, 7, kwin_ref, kwinb_ref)):
        for g in range(G):
            k = nr_full(col(ok + g * hd), g_ref[grow:grow + 1, :])
            f_ref[:, g * hd:(g + 1) * hd] = k
            b_ref[:, g * hd:(g + 1) * hd] = k.astype(BF16)
            v = col(ov + g * hd)
            f_ref[:, (G + g) * hd:(G + g + 1) * hd] = v
            b_ref[:, (G + g) * hd:(G + g + 1) * hd] = v.astype(BF16)
    gd = col(o_dg)
    gd_ref[...] = 1.0 / (1.0 + jnp.exp(-gd))


def qkprep(proj, cs, gains, *, d_model):
    B, S, NP = proj.shape
    offs = tuple(_col_offsets(d_model))
    mix = d_model // N_MIXERS
    H = mix // HEAD_DIM
    dkv = (H // 2) * HEAD_DIM
    nkv = NSA_GROUPS * HEAD_DIM
    tm = _tile(S, 128)
    nbp = _round_up(-(-S // MOBA_BLOCK), LANES)

    def o(w, dt):
        return jax.ShapeDtypeStruct((B, S, w), dt), pl.BlockSpec((None, tm, w), lambda b, i: (b, i, 0))

    outs = [o(mix, BF16), o(2 * mix, F32), o(2 * mix, BF16),
            (jax.ShapeDtypeStruct((B, nbp, mix), F32), pl.BlockSpec((None, nbp, mix), lambda b, i: (b, 0, 0))),
            o(2 * mix, BF16), o(2 * dkv, F32), o(2 * dkv, BF16),
            o(mix, BF16), o(2 * mix, F32), o(2 * mix, BF16),
            o(mix, BF16), o(2 * nkv, F32), o(2 * nkv, BF16), o(2 * nkv, F32), o(2 * nkv, BF16),
            o(2 * nkv, F32), o(2 * nkv, BF16), o(LANES, F32)]
    res = pl.pallas_call(
        functools.partial(_qkprep_kernel, offs=offs, n_heads=H, tm=tm),
        out_shape=[s for s, _ in outs],
        grid=(B, S // tm),
        in_specs=[pl.BlockSpec((None, tm, NP), lambda b, i: (b, i, 0)),
                  pl.BlockSpec((tm, 4 * HEAD_DIM), lambda b, i: (i, 0)),
                  pl.BlockSpec((8, HEAD_DIM), lambda b, i: (0, 0))],
        out_specs=[s for _, s in outs],
        compiler_params=_cparams(("parallel", "arbitrary"), 48),
        name="qkprep",
    )(proj, cs, gains)
    keys = ("qa", "kva", "kvab", "kma", "qb", "kvb", "kvbb", "qc", "kvc", "kvcb",
            "qd", "kcmp", "kcmpb", "ksel", "kselb", "kwin", "kwinb", "gd")
    return dict(zip(keys, res))


GATHER_PAGES_PER_STEP = 4


def _gather_kernel(tbl_ref, layer_ref, *refs, n_pages, pps, n_slots, page, with_km):
    pool_refs = refs[:pps]
    tail_ref, o_ref = refs[pps], refs[pps + 1]
    s = pl.program_id(1)
    hd = HEAD_DIM
    if with_km:
        km_ref = refs[pps + 2]

        @pl.when(s == 0)
        def _():
            km_ref[...] = jnp.zeros_like(km_ref)

    for k in range(pps):
        pg = s * pps + k
        rows = pl.ds(k * page, page)

        @pl.when(pg < n_pages)
        def _(k=k, pg=pg, rows=rows):
            for c in range(n_slots):
                x = pool_refs[k][pl.ds(c, page, stride=n_slots), :]
                o_ref[rows, c * hd:(c + 1) * hd] = x.astype(o_ref.dtype)
                if with_km and c < n_slots // 2:
                    ksum = jnp.sum(x, axis=0, keepdims=True) * (1.0 / MOBA_BLOCK)
                    blk_row = lax.broadcasted_iota(jnp.int32, (km_ref.shape[0], hd), 0)
                    km_ref[:, c * hd:(c + 1) * hd] += jnp.where(blk_row == (pg * page) // MOBA_BLOCK, ksum, 0.0)

        @pl.when(pg == n_pages)
        def _(rows=rows):
            row = lax.broadcasted_iota(jnp.int32, (page, o_ref.shape[1]), 0)
            o_ref[rows, :] = jnp.where(row == 0, tail_ref[0:1, :], 0.0).astype(o_ref.dtype)

        @pl.when(pg > n_pages)
        def _(rows=rows):
            o_ref[rows, :] = jnp.zeros((page, o_ref.shape[1]), o_ref.dtype)


def gather_cache(cache, l, page_table, tail, *, lp, out_dtype, with_kmean=False):
    depth, n_pool, page = cache.shape[:3]
    n_slots = cache.shape[3] * cache.shape[4]
    hd = cache.shape[5]
    W = n_slots * hd
    pool = cache.reshape(depth, n_pool, page * n_slots, hd)
    B, n_pages = page_table.shape
    pps = GATHER_PAGES_PER_STEP
    assert lp % (pps * page) == 0
    out_shape = [jax.ShapeDtypeStruct((B, lp, W), out_dtype)]
    out_specs = [pl.BlockSpec((None, pps * page, W), lambda b, s, tbl, lix: (b, s, 0))]
    if with_kmean:
        nbp = _round_up(-(-lp // MOBA_BLOCK), LANES)
        out_shape.append(jax.ShapeDtypeStruct((B, nbp, W // 2), F32))
        out_specs.append(pl.BlockSpec((None, nbp, W // 2), lambda b, s, tbl, lix: (b, 0, 0)))

    def page_spec(k):
        return pl.BlockSpec((None, None, page * n_slots, hd),
                            lambda b, s, tbl, lix: (lix[0], tbl[b, jnp.minimum(s * pps + k, n_pages - 1)], 0, 0))

    grid_spec = pltpu.PrefetchScalarGridSpec(
        num_scalar_prefetch=2,
        grid=(B, lp // (pps * page)),
        in_specs=[page_spec(k) for k in range(pps)]
        + [pl.BlockSpec((None, tail.shape[1], W), lambda b, s, tbl, lix: (b, 0, 0))],
        out_specs=out_specs,
    )
    res = pl.pallas_call(
        functools.partial(_gather_kernel, n_pages=n_pages, pps=pps, n_slots=n_slots, page=page, with_km=with_kmean),
        out_shape=out_shape,
        grid_spec=grid_spec,
        compiler_params=_cparams(("parallel", "arbitrary"), 40),
        name="gather_cache",
    )(page_table, jnp.asarray([l], jnp.int32), *([pool] * pps), tail)
    return res if with_kmean else res[0]


def _rank_select(score, cand, n_iota, n_real, n_keep):
    sm = jnp.where(cand, score, -jnp.inf)
    rank = jnp.zeros(score.shape, F32)
    for m in range(n_real):
        c = sm[:, m:m + 1]
        beats = (c > sm) | ((c == sm) & (m < n_iota))
        rank = rank + jnp.where(beats, 1.0, 0.0)
    return cand & (rank < n_keep)


def _moba_sel_kernel(q_ref, km_ref, o_ref, *, tq, n_heads, nb, nbp, pos0):
    i = pl.program_id(1)
    hd = HEAD_DIM
    pos = pos0 + i * tq + lax.broadcasted_iota(jnp.int32, (tq, 1), 0)
    own = pos >> _log2(MOBA_BLOCK)
    n_iota = lax.broadcasted_iota(jnp.int32, (tq, nbp), 1)
    cand = n_iota < own
    for h in range(n_heads):
        q = q_ref[:, h * hd:(h + 1) * hd]
        km = km_ref[:, h * hd:(h + 1) * hd].astype(BF16)
        gate = lax.dot_general(q, km, (((1,), (1,)), ((), ())), preferred_element_type=F32)
        sel = _rank_select(gate, cand, n_iota, nb, MOBA_TOPK) | (n_iota == own)
        o_ref[:, h * nbp:(h + 1) * nbp] = jnp.where(sel, 1.0, 0.0).astype(o_ref.dtype)


def moba_select(q, kmean, *, nb, pos0, tq):
    B, S, W = q.shape
    nbp = kmean.shape[1]
    H = W // HEAD_DIM
    return pl.pallas_call(
        functools.partial(_moba_sel_kernel, tq=tq, n_heads=H, nb=nb, nbp=nbp, pos0=pos0),
        out_shape=jax.ShapeDtypeStruct((B, S, H * nbp), BF16),
        grid=(B, S // tq),
        in_specs=[pl.BlockSpec((None, tq, W), lambda b, i: (b, i, 0)),
                  pl.BlockSpec((None, nbp, W), lambda b, i: (b, 0, 0))],
        out_specs=pl.BlockSpec((None, tq, H * nbp), lambda b, i: (b, i, 0)),
        compiler_params=_cparams(("parallel", "parallel"), 32),
        name="moba_select",
    )(q, kmean)


def _compress_kernel(x_ref, pos_ref, w1_ref, w2_ref, o_ref, acc_ref, *, nr):
    r = pl.program_id(1)
    hd = HEAD_DIM
    G = NSA_GROUPS

    @pl.when(r == 0)
    def _():
        acc_ref[...] = jnp.zeros_like(acc_ref)

    for kv in range(2):
        pe = pos_ref[kv]
        w = w1_ref[kv]
        for g in range(G):
            c = (kv * G + g) * hd
            xs = (x_ref[:, c:c + hd] + pe).astype(BF16)
            acc_ref[kv * G + g] += jnp.dot(xs, w, preferred_element_type=F32)

    @pl.when(r == nr - 1)
    def _():
        for kv in range(2):
            for g in range(G):
                a = acc_ref[kv * G + g]
                h = 0.5 * a * (1.0 + jnp.tanh(0.7978845608028654 * (a + 0.044715 * a * a * a)))
                c = (kv * G + g) * hd
                o_ref[:, c:c + hd] = jnp.dot(h.astype(BF16), w2_ref[kv], preferred_element_type=F32)


def nsa_compress(x, pos_emb, w1, w2):
    R = x.shape[0]
    hd = HEAD_DIM
    wrow = 2 * NSA_GROUPS * hd
    nr = x.shape[1] // wrow
    hid = w1.shape[2]
    tr = _tile(R, 1280)
    pos4 = pos_emb.reshape(2, nr, 1, hd).astype(F32)
    return pl.pallas_call(
        functools.partial(_compress_kernel, nr=nr),
        out_shape=jax.ShapeDtypeStruct((R, wrow), F32),
        grid=(R // tr, nr),
        in_specs=[pl.BlockSpec((tr, wrow), lambda t, r: (t, r)),
                  pl.BlockSpec((2, None, 1, hd), lambda t, r: (0, r, 0, 0)),
                  pl.BlockSpec((2, hd, hid), lambda t, r: (0, r, 0)),
                  pl.BlockSpec((2, hid, hd), lambda t, r: (0, 0, 0))],
        out_specs=pl.BlockSpec((tr, wrow), lambda t, r: (t, 0)),
        scratch_shapes=[pltpu.VMEM((2 * NSA_GROUPS, tr, hid), F32)],
        compiler_params=_cparams(("parallel", "arbitrary"), 32),
        name="nsa_compress",
    )(x, pos4, w1, w2)


def _nsa_cmp_kernel(q_ref, kc_ref, o_ref, sel_ref, *, tq, rep, nc, ncp, pos0, scale):
    i = pl.program_id(1)
    hd = HEAD_DIM
    G = NSA_GROUPS
    pos = _row_positions(pos0, i, tq, 1)
    own = pos >> _log2(NSA_BLOCK)
    n_iota = lax.broadcasted_iota(jnp.int32, (tq, ncp), 1)
    cand = (n_iota < own) & (n_iota < nc)
    pos_r = _row_positions(pos0, i, tq, rep)
    n_iota_r = lax.broadcasted_iota(jnp.int32, (rep * tq, ncp), 1)
    vis_r = ((n_iota_r + 1) * NSA_BLOCK <= pos_r + 1) & (n_iota_r < nc)
    for g in range(G):
        q = jnp.concatenate([q_ref[:, (g * rep + r) * hd:(g * rep + r + 1) * hd] for r in range(rep)], axis=0)
        kc = kc_ref[:, g * hd:(g + 1) * hd].astype(BF16)
        vc = kc_ref[:, (G + g) * hd:(G + g + 1) * hd].astype(BF16)
        s = lax.dot_general(q, kc, (((1,), (1,)), ((), ())), preferred_element_type=F32) * scale
        s = jnp.where(vis_r, s, NEG_INF)
        e = jnp.exp(s - jnp.max(s, axis=-1, keepdims=True))
        p = jnp.where(vis_r, e / jnp.sum(e, axis=-1, keepdims=True), 0.0)
        o = jnp.dot(p.astype(BF16), vc, preferred_element_type=F32)
        imp = p[0:tq]
        for r in range(rep):
            o_ref[:, (g * rep + r) * hd:(g * rep + r + 1) * hd] = o[r * tq:(r + 1) * tq].astype(o_ref.dtype)
            if r:
                imp = imp + p[r * tq:(r + 1) * tq]
        sel = _rank_select(imp, cand, n_iota, nc, NSA_N_SEL - 1) | (n_iota == own)
        sel_ref[:, g * ncp:(g + 1) * ncp] = jnp.where(sel, 1.0, 0.0).astype(sel_ref.dtype)


def nsa_cmp_attend(q, kc, *, nc, pos0, tq):
    B, S, W = q.shape
    ncp = kc.shape[1]
    H = W // HEAD_DIM
    rep = H // NSA_GROUPS
    return pl.pallas_call(
        functools.partial(_nsa_cmp_kernel, tq=tq, rep=rep, nc=nc, ncp=ncp, pos0=pos0, scale=HEAD_DIM ** -0.5),
        out_shape=[jax.ShapeDtypeStruct((B, S, W), BF16), jax.ShapeDtypeStruct((B, S, NSA_GROUPS * ncp), BF16)],
        grid=(B, S // tq),
        in_specs=[pl.BlockSpec((None, tq, W), lambda b, i: (b, i, 0)),
                  pl.BlockSpec((None, ncp, kc.shape[2]), lambda b, i: (b, 0, 0))],
        out_specs=[pl.BlockSpec((None, tq, W), lambda b, i: (b, i, 0)),
                   pl.BlockSpec((None, tq, NSA_GROUPS * ncp), lambda b, i: (b, i, 0))],
        compiler_params=_cparams(("parallel", "parallel"), 32),
        name="nsa_cmp_attend",
    )(q, kc)


def _fa_bounds(i, *, tq, tk, nkt, pos0, kpos0, window, minimum, maximum):
    last = pos0 + (i + 1) * tq - 1 - kpos0
    hi = minimum(last // tk, nkt - 1)
    if window is None:
        return 0, hi
    first = pos0 + i * tq - window + 1 - kpos0
    return maximum(first // tk, 0), hi


def _tile_steps(spans):
    cols = []
    for i, (lo, hi) in enumerate(spans):
        for kt in range(hi, lo - 1, -1):
            cols.append((i, kt, int(kt == hi), int(kt == lo)))
    return jnp.asarray(np.array(cols, np.int32).T)


def _fa_kernel(st_ref, *refs, tq, tk, hkv, rep, pos0, kpos0, n_keys, window, blk, nbp, scale, has_sel,
               self_in_first_tile):
    if has_sel:
        q_ref, k_ref, v_ref, sel_ref, o_ref, m_sc, l_sc, acc_sc = refs
    else:
        q_ref, k_ref, v_ref, o_ref, m_sc, l_sc, acc_sc = refs
    t = pl.program_id(1)
    i = st_ref[0, t]
    kt = st_ref[1, t]
    hd = HEAD_DIM

    @pl.when(st_ref[2, t] == 1)
    def _():
        m_sc[...] = jnp.full(m_sc.shape, NEG_INF, F32)
        l_sc[...] = jnp.zeros_like(l_sc)
        acc_sc[...] = jnp.zeros_like(acc_sc)

    qpos = _row_positions(pos0, i, tq, rep)
    kidx = kt * tk + lax.broadcasted_iota(jnp.int32, (1, tk), 1)
    kpos = kpos0 + kidx
    base = (kpos <= qpos) & (kidx < n_keys)
    if window is not None:
        base = base & (kpos > qpos - window)
    if has_sel:
        n_of_key = (kt * tk + lax.broadcasted_iota(jnp.int32, (nbp, tk), 1)) >> _log2(blk)
        expand = jnp.where(n_of_key == lax.broadcasted_iota(jnp.int32, (nbp, tk), 0), 1.0, 0.0).astype(BF16)
    for g in range(hkv):
        q = jnp.concatenate([q_ref[:, (g * rep + r) * hd:(g * rep + r + 1) * hd] for r in range(rep)], axis=0)
        k = k_ref[:, g * hd:(g + 1) * hd]
        v = v_ref[:, g * hd:(g + 1) * hd]
        s = lax.dot_general(q, k, (((1,), (1,)), ((), ())), preferred_element_type=F32) * scale
        ok = base
        if has_sel:
            picked = jnp.dot(sel_ref[:, g * nbp:(g + 1) * nbp], expand, preferred_element_type=F32)
            ok = ok & (jnp.concatenate([picked] * rep, axis=0) > 0.5)
        s = jnp.where(ok, s, NEG_INF)
        m_prev = m_sc[g]
        m_new = jnp.maximum(m_prev, jnp.max(s, axis=-1, keepdims=True))
        p = jnp.exp(s - m_new)
        if not self_in_first_tile:
            p = jnp.where(ok, p, 0.0)
        alpha = jnp.exp(m_prev - m_new)
        l_sc[g] = alpha * l_sc[g] + jnp.sum(p, axis=-1, keepdims=True)
        acc_sc[g] = alpha * acc_sc[g] + jnp.dot(p.astype(BF16), v, preferred_element_type=F32)
        m_sc[g] = m_new

    @pl.when(st_ref[3, t] == 1)
    def _():
        for g in range(hkv):
            l = l_sc[g]
            o = acc_sc[g] / jnp.where(l > 0.0, l, 1.0)
            for r in range(rep):
                o_ref[:, (g * rep + r) * hd:(g * rep + r + 1) * hd] = o[r * tq:(r + 1) * tq].astype(o_ref.dtype)


def flash_attend(q, kv, sel=None, *, hkv, rep, pos0, kpos0, n_keys, window=None, blk=1, scale, out_dtype, tq, tk):
    B, S, Wq = q.shape
    Lp = kv.shape[1]
    hw = hkv * HEAD_DIM
    nq, nkt = S // tq, Lp // tk
    kw = dict(tq=tq, tk=tk, nkt=nkt, pos0=pos0, kpos0=kpos0, window=window)
    steps = _tile_steps([_fa_bounds(i, minimum=min, maximum=max, **kw) for i in range(nq)])

    self_in_first_tile = (pos0 - kpos0) % tk + tq <= tk

    in_specs = [pl.BlockSpec((None, tq, Wq), lambda b, t, st: (b, st[0, t], 0)),
                pl.BlockSpec((None, tk, hw), lambda b, t, st: (b, st[1, t], 0)),
                pl.BlockSpec((None, tk, hw), lambda b, t, st: (b, st[1, t], 1))]
    args = [q, kv, kv]
    nbp = 0
    if sel is not None:
        nbp = sel.shape[2] // hkv
        in_specs.append(pl.BlockSpec((None, tq, hkv * nbp), lambda b, t, st: (b, st[0, t], 0)))
        args.append(sel)
    rows = rep * tq
    return pl.pallas_call(
        functools.partial(_fa_kernel, tq=tq, tk=tk, hkv=hkv, rep=rep, pos0=pos0, kpos0=kpos0, n_keys=n_keys,
                          window=window, blk=blk, nbp=nbp, scale=scale, has_sel=sel is not None,
                          self_in_first_tile=self_in_first_tile),
        out_shape=jax.ShapeDtypeStruct((B, S, Wq), out_dtype),
        grid_spec=pltpu.PrefetchScalarGridSpec(
            num_scalar_prefetch=1, grid=(B, steps.shape[1]), in_specs=in_specs,
            out_specs=pl.BlockSpec((None, tq, Wq), lambda b, t, st: (b, st[0, t], 0)),
            scratch_shapes=[pltpu.VMEM((hkv, rows, 1), F32), pltpu.VMEM((hkv, rows, 1), F32),
                            pltpu.VMEM((hkv, rows, HEAD_DIM), F32)]),
        compiler_params=_cparams(("parallel", "arbitrary"), 40),
        name="flash_attend",
    )(steps, *args)


def _sb_kernel(st_ref, q_ref, k_ref, v_ref, o_ref, c_sc, acc_sc, *, tq, tk, sub, n_heads, pos0, n_keys, scale,
               stack_heads):
    t = pl.program_id(1)
    i = st_ref[0, t]
    kt = st_ref[1, t]
    hd = HEAD_DIM

    @pl.when(st_ref[2, t] == 1)
    def _():
        c_sc[...] = jnp.zeros_like(c_sc)
        acc_sc[...] = jnp.zeros_like(acc_sc)

    def step():
        qpos = pos0 + i * tq + lax.broadcasted_iota(jnp.int32, (tq, 1), 0)
        suffix = jnp.where(lax.broadcasted_iota(jnp.int32, (sub, sub), 0) >= lax.broadcasted_iota(jnp.int32, (sub, sub), 1),
                           1.0, 0.0).astype(BF16)
        def log_sigmoid(z):
            return jnp.minimum(z, 0.0) - jnp.log(1.0 + jnp.exp(-jnp.abs(z)))

        def suffix_sum(x):
            x_hi = x.astype(BF16)
            x_lo = (x - x_hi.astype(F32)).astype(BF16)
            return (jnp.dot(x_hi, suffix, preferred_element_type=F32)
                    + jnp.dot(x_lo, suffix, preferred_element_type=F32))

        if stack_heads:
            qpos_all = _row_positions(pos0, i, tq, n_heads)
            c = jnp.concatenate([c_sc[h] for h in range(n_heads)], axis=0)
            accs = [acc_sc[h] for h in range(n_heads)]
            for j in reversed(range(tk // sub)):
                kidx = kt * tk + j * sub + lax.broadcasted_iota(jnp.int32, (1, sub), 1)
                past = (kidx < qpos_all) & (kidx < n_keys)
                z = jnp.concatenate(
                    [lax.dot_general(q_ref[:, h * hd:(h + 1) * hd], k_ref[j * sub:(j + 1) * sub, h * hd:(h + 1) * hd],
                                     (((1,), (1,)), ((), ())), preferred_element_type=F32) for h in range(n_heads)],
                    axis=0) * scale
                log_sig = log_sigmoid(z)
                log_keep = jnp.where(past, log_sig - z, 0.0)
                incl = suffix_sum(log_keep)
                a = jnp.where(past, jnp.exp(log_sig + c + incl - log_keep), 0.0).astype(BF16)
                for h in range(n_heads):
                    accs[h] = accs[h] + jnp.dot(a[h * tq:(h + 1) * tq], v_ref[j * sub:(j + 1) * sub, h * hd:(h + 1) * hd],
                                                preferred_element_type=F32)
                c = c + incl[:, 0:1]
            for h in range(n_heads):
                acc_sc[h] = accs[h]
                c_sc[h] = c[h * tq:(h + 1) * tq]
            return

        pasts = []
        for j in range(tk // sub):
            kidx = kt * tk + j * sub + lax.broadcasted_iota(jnp.int32, (1, sub), 1)
            pasts.append((kidx < qpos) & (kidx < n_keys))
        for h in range(n_heads):
            q = q_ref[:, h * hd:(h + 1) * hd]
            c = c_sc[h]
            acc = acc_sc[h]
            for j in reversed(range(tk // sub)):
                past = pasts[j]
                k = k_ref[j * sub:(j + 1) * sub, h * hd:(h + 1) * hd]
                v = v_ref[j * sub:(j + 1) * sub, h * hd:(h + 1) * hd]
                z = lax.dot_general(q, k, (((1,), (1,)), ((), ())), preferred_element_type=F32) * scale
                log_sig = log_sigmoid(z)
                log_keep = jnp.where(past, log_sig - z, 0.0)
                incl = suffix_sum(log_keep)
                log_after = c + incl - log_keep
                a = jnp.where(past, jnp.exp(log_sig + log_after), 0.0)
                acc = acc + jnp.dot(a.astype(BF16), v, preferred_element_type=F32)
                c = c + incl[:, 0:1]
            acc_sc[h] = acc
            c_sc[h] = c

    step()

    @pl.when(st_ref[3, t] == 1)
    def _():
        for h in range(n_heads):
            o_ref[:, h * hd:(h + 1) * hd] = acc_sc[h].astype(o_ref.dtype)


SB_SUB_TILE = 256


def sb_attend(q, kv, *, pos0, n_keys, tq, tk):
    B, S, W = q.shape
    Lp = kv.shape[1]
    H = W // HEAD_DIM
    nq, nkt = S // tq, Lp // tk

    steps = _tile_steps([(0, min((pos0 + (i + 1) * tq - 1) // tk, nkt - 1)) for i in range(nq)])
    return pl.pallas_call(
        functools.partial(_sb_kernel, tq=tq, tk=tk, sub=min(tk, SB_SUB_TILE), n_heads=H, pos0=pos0, n_keys=n_keys,
                          scale=HEAD_DIM ** -0.5, stack_heads=H * tq <= SB_SUB_TILE),
        out_shape=jax.ShapeDtypeStruct((B, S, W), BF16),
        grid_spec=pltpu.PrefetchScalarGridSpec(
            num_scalar_prefetch=1, grid=(B, steps.shape[1]),
            in_specs=[pl.BlockSpec((None, tq, W), lambda b, t, st: (b, st[0, t], 0)),
                      pl.BlockSpec((None, tk, W), lambda b, t, st: (b, st[1, t], 0)),
                      pl.BlockSpec((None, tk, W), lambda b, t, st: (b, st[1, t], 1))],
            out_specs=pl.BlockSpec((None, tq, W), lambda b, t, st: (b, st[0, t], 0)),
            scratch_shapes=[pltpu.VMEM((H, tq, 1), F32), pltpu.VMEM((H, tq, HEAD_DIM), F32)]),
        compiler_params=_cparams(("parallel", "arbitrary"), 40),
        name="sb_attend",
    )(steps, q, kv, kv)


def _post_kernel(oa_ref, ob_ref, oc_ref, ocmp_ref, osel_ref, owin_ref, gd_ref, lam_ref, sub_ref, li_ref, out_ref,
                 *, n_heads):
    hd = HEAD_DIM
    H = n_heads
    mix = H * hd
    lv = lam_ref[...]
    lam_init = li_ref[:, 0:1]
    lam = (jnp.exp(jnp.sum(lv[0:1] * lv[1:2], axis=-1, keepdims=True))
           - jnp.exp(jnp.sum(lv[2:3] * lv[3:4], axis=-1, keepdims=True)) + lam_init)
    out_ref[:, 0:mix] = oa_ref[...]
    for h in range(H):
        d = ob_ref[:, (2 * h) * hd:(2 * h + 1) * hd] - lam * ob_ref[:, (2 * h + 1) * hd:(2 * h + 2) * hd]
        ms = jnp.mean(d * d, axis=-1, keepdims=True)
        y = d * lax.rsqrt(ms + EPS) * sub_ref[...]
        out_ref[:, mix + h * hd:mix + (h + 1) * hd] = (y * (1.0 - lam_init)).astype(out_ref.dtype)
    out_ref[:, 2 * mix:3 * mix] = oc_ref[...]
    gd = gd_ref[...]
    for h in range(H):
        sl = slice(h * hd, (h + 1) * hd)
        o = (ocmp_ref[:, sl].astype(F32) * gd[:, 3 * h:3 * h + 1]
             + osel_ref[:, sl].astype(F32) * gd[:, 3 * h + 1:3 * h + 2]
             + owin_ref[:, sl].astype(F32) * gd[:, 3 * h + 2:3 * h + 3])
        out_ref[:, 3 * mix + h * hd:3 * mix + (h + 1) * hd] = o.astype(out_ref.dtype)


def mixer_post(oa, ob, oc, ocmp, osel, owin, gd, diff_lambda, subln_g, lam_init):
    M, mix = oa.shape
    H = mix // HEAD_DIM
    tm = _tile(M, 256)
    li = jnp.full((1, LANES), lam_init, F32)

    def row(w):
        return pl.BlockSpec((tm, w), lambda i: (i, 0))

    def whole(shape):
        return pl.BlockSpec(shape, lambda i: (0,) * len(shape))

    return pl.pallas_call(
        functools.partial(_post_kernel, n_heads=H),
        out_shape=jax.ShapeDtypeStruct((M, N_MIXERS * mix), BF16),
        grid=(M // tm,),
        in_specs=[row(mix), row(2 * mix), row(mix), row(mix), row(mix), row(mix), row(LANES),
                  whole(diff_lambda.shape), whole((1, HEAD_DIM)), whole((1, LANES))],
        out_specs=row(N_MIXERS * mix),
        compiler_params=_cparams(("parallel",), 32),
        name="mixer_post",
    )(oa, ob, oc, ocmp, osel, owin, gd, diff_lambda.astype(F32), subln_g.reshape(1, HEAD_DIM).astype(F32), li)


def _rope_tables(pos):
    def tab(width):
        half = width // 2
        inv = ROPE_THETA ** (-jnp.arange(half, dtype=F32) / half)
        ang = pos[:, None] * inv[None, :]
        c, s = jnp.cos(ang), jnp.sin(ang)
        reps = HEAD_DIM // width
        return jnp.tile(jnp.concatenate([c, c], axis=1), (1, reps)), jnp.tile(jnp.concatenate([-s, s], axis=1), (1, reps))
    c1, s1 = tab(HEAD_DIM)
    c2, s2 = tab(HEAD_DIM // 2)
    return jnp.concatenate([c1, s1, c2, s2], axis=1)


DECODE_KEY_TILE = 1024


def _ffn(x, g, W, l, j):
    xn = rmsnorm_cast(x, g)
    h = ffn_gateup(xn, W["wgu"], (l, j), tm=1024, tn=256)
    return matmul(h, W["wdn"], (l, j), n_out=x.shape[1], tm=512, tn=256, resid=x, scale=0.5, name="ffn_down")


def _mixers(pre, *, B, S, d_model, pos0, past, page_table, lw, li):
    mix = d_model // N_MIXERS
    H = mix // HEAD_DIM
    decode = past is not None
    scale = HEAD_DIM ** -0.5

    if not decode:
        sq = S
        q = {k: pre[k] for k in ("qa", "qb", "qc", "qd")}
        kva, kvb, kvc, ksel, kwin = pre["kvab"], pre["kvbb"], pre["kvcb"], pre["kselb"], pre["kwinb"]
        kmean = pre["kma"]
        cmp_src = pre["kcmp"]
        n_keys = S
        lp = S
        win_kpos0, win_keys = 0, S
        tq, tk = _tile(S, 256), _tile(S, 256)
        tk_win = tk
    else:
        sq = BF16_SUBLANES
        nb_seq = B

        def as_rows(a, dt):
            a = a.reshape(nb_seq, 1, a.shape[-1]).astype(dt)
            return jnp.pad(a, ((0, 0), (0, sq - 1), (0, 0)))

        q = {k: as_rows(pre[k], BF16) for k in ("qa", "qb", "qc", "qd")}
        n_past = page_table.shape[1] * past["moba"].shape[2]
        n_keys = n_past + 1
        tk = DECODE_KEY_TILE
        lp = _round_up(n_keys, tk)

        def tail(name):
            return as_rows(pre[name], F32)[:, :8]

        def gather(cache, name, dt, **kw):
            return gather_cache(past[cache], li, page_table, tail(name), lp=lp, out_dtype=dt, **kw)

        kva, kmean = gather("moba", "kva", BF16, with_kmean=True)
        kvb = gather("diff", "kvb", BF16)
        kvc = gather("sb", "kvc", BF16)
        cmp_src = gather("cmp", "kcmp", F32)
        ksel = gather("sel", "ksel", BF16)
        win_full = past["win_full"]
        win_keys = win_full.shape[1]
        win_kpos0 = pos0 + 1 - win_keys
        tk_win = 256
        kwin = jnp.pad(win_full, ((0, 0), (0, _round_up(win_keys, tk_win) - win_keys), (0, 0))).astype(BF16)
        tq = sq

    nb = -(-n_keys // MOBA_BLOCK)
    sel_a = moba_select(q["qa"], kmean, nb=nb, pos0=pos0, tq=tq)
    oa = flash_attend(q["qa"], kva, sel_a, hkv=H, rep=1, pos0=pos0, kpos0=0, n_keys=n_keys, blk=MOBA_BLOCK,
                      scale=scale, out_dtype=BF16, tq=tq, tk=tk)
    ob = flash_attend(q["qb"], kvb, hkv=H // 2, rep=4, pos0=pos0, kpos0=0, n_keys=n_keys,
                      scale=(HEAD_DIM // 2) ** -0.5, out_dtype=F32, tq=tq, tk=tk)
    oc = sb_attend(q["qc"], kvc, pos0=pos0, n_keys=n_keys, tq=tq, tk=tk)
    nblk = lp // NSA_BLOCK
    nc = -(-n_keys // NSA_BLOCK)
    kc = nsa_compress(cmp_src.reshape(B * nblk, -1), lw["cmp_pos"], lw["cmp_w1"], lw["cmp_w2"])
    ncp = _round_up(nblk, LANES)
    kc = jnp.pad(kc.reshape(B, nblk, -1), ((0, 0), (0, ncp - nblk), (0, 0)))
    ocmp, sel_d = nsa_cmp_attend(q["qd"], kc, nc=nc, pos0=pos0, tq=tq)
    rep_d = H // NSA_GROUPS
    osel = flash_attend(q["qd"], ksel, sel_d, hkv=NSA_GROUPS, rep=rep_d, pos0=pos0, kpos0=0, n_keys=n_keys,
                        blk=NSA_BLOCK, scale=scale, out_dtype=BF16, tq=tq, tk=tk)
    owin = flash_attend(q["qd"], kwin, hkv=NSA_GROUPS, rep=rep_d, pos0=pos0, kpos0=win_kpos0, n_keys=win_keys,
                        window=NSA_WINDOW, scale=scale, out_dtype=BF16, tq=tq, tk=tk_win)

    def rows(a):
        return a[:, 0, :] if decode else a.reshape(B * S, a.shape[-1])

    gd = pre["gd"].reshape(-1, LANES)
    return mixer_post(rows(oa), rows(ob), rows(oc), rows(ocmp), rows(osel), rows(owin), gd,
                      lw["diff_lambda"], lw["diff_subln_g"], 0.8 - 0.6 * math.exp(-0.3 * li))


def _layer(x, *, B, S, pos0, past, page_table, W, lw, li):
    M, D = x.shape
    decode = past is not None
    x = _ffn(x, lw["norm_g"][0], W, li, 0)
    xn = rmsnorm_cast(x, lw["norm_g"][1])
    proj = matmul(xn, W["w_proj"], (li,), n_out=W["w_proj"].shape[2], tm=1024, tn=256, name="in_proj")
    if decode:
        pos = jnp.full((M,), pos0, F32)
        pre = qkprep(proj.reshape(1, M, -1), _rope_tables(pos), lw["gains"], d_model=D)
    else:
        pos = (pos0 + jnp.arange(S)).astype(F32)
        pre = qkprep(proj.reshape(B, S, -1), _rope_tables(pos), lw["gains"], d_model=D)
    new_rows = {k: pre[k].reshape(B, S, -1) for k in ("kva", "kvb", "kvc", "kcmp", "ksel", "kwin")}
    if decode:
        past = dict(past)
        past["win_full"] = jnp.concatenate([past["win"].reshape(B, past["win"].shape[1], -1), new_rows["kwin"]], axis=1)
        win_all = past["win_full"]
    else:
        win_all = new_rows["kwin"]
    n_state = min(NSA_WINDOW, pos0 + S)
    win_state = win_all[:, win_all.shape[1] - n_state:]
    o_all = _mixers(pre, B=B, S=S, d_model=D, pos0=pos0, past=past, page_table=page_table, lw=lw, li=li)
    merged = merge(xn, o_all, W["w_mg"], W["w_br"], li, tm=512, tn=256)
    x = matmul(merged, W["w_out"], (li,), n_out=D, tm=1024, tn=512, resid=x, scale=1.0, name="out_proj")
    x = _ffn(x, lw["norm_g"][2], W, li, 1)
    return x, (new_rows["kva"], new_rows["kvb"], new_rows["kvc"], new_rows["kcmp"], new_rows["ksel"], win_state)


def kernel(x_prompt, x_sample, cache_moba_kv, cache_diff_kv, cache_sb_kv, cache_nsa_cmp_kv, cache_nsa_sel_kv,
           state_nsa_win_kv, page_table, norm_g, w_ffn_gate_up, w_ffn_down, w_in, moba_qk_g, diff_qk_g,
           diff_lambda, diff_subln_g, nsa_qk_g, nsa_cmp_pos, nsa_cmp_w1, nsa_cmp_w2, w_branch, w_out):
    bp, sp, D = x_prompt.shape
    bs, ss, _ = x_sample.shape
    assert ss == 1, "the decode group is one new row per sequence"
    depth = norm_g.shape[0]
    page = cache_moba_kv.shape[2]
    past_len = page_table.shape[1] * page
    offs = _col_offsets(D)
    o_mg = offs[17]
    n_proj = _round_up(o_mg, 256)
    H = D // N_MIXERS // HEAD_DIM
    heads = {"kva": H, "kvb": H // 2, "kvc": H, "kcmp": NSA_GROUPS, "ksel": NSA_GROUPS, "kwin": NSA_GROUPS}

    yp = x_prompt.reshape(bp * sp, D)
    ys = x_sample.reshape(bs * ss, D)
    rows_p = [[] for _ in range(6)]
    rows_s = [[] for _ in range(6)]
    W = {
        "wgu": w_ffn_gate_up,
        "wdn": w_ffn_down.astype(BF16),
        "w_proj": w_in[:, :, :n_proj].astype(BF16),
        "w_mg": w_in[:, :, o_mg:].astype(BF16),
        "w_br": w_branch.astype(BF16),
        "w_out": w_out.astype(BF16),
    }
    caches = {"moba": cache_moba_kv, "diff": cache_diff_kv, "sb": cache_sb_kv,
              "cmp": cache_nsa_cmp_kv, "sel": cache_nsa_sel_kv}
    for l in range(depth):
        half_tile = jnp.tile(diff_qk_g[l].astype(F32), (1, 2))
        lw = {
            "norm_g": norm_g[l],
            "gains": jnp.concatenate([moba_qk_g[l].astype(F32), half_tile, nsa_qk_g[l].astype(F32)], axis=0),
            "diff_lambda": diff_lambda[l],
            "diff_subln_g": diff_subln_g[l],
            "cmp_pos": nsa_cmp_pos[l],
            "cmp_w1": nsa_cmp_w1[l].astype(BF16),
            "cmp_w2": nsa_cmp_w2[l].astype(BF16),
        }
        past_s = dict(caches, win=state_nsa_win_kv[l])
        yp, new_p = _layer(yp, B=bp, S=sp, pos0=0, past=None, page_table=None, W=W, lw=lw, li=l)
        ys, new_s = _layer(ys, B=bs, S=ss, pos0=past_len, past=past_s, page_table=page_table, W=W, lw=lw, li=l)
        for i in range(6):
            rows_p[i].append(new_p[i])
            rows_s[i].append(new_s[i])

    def stack(rows, name):
        a = jnp.stack(rows, axis=0)
        return a.reshape(a.shape[:3] + (2, heads[name], HEAD_DIM))

    names = ("kva", "kvb", "kvc", "kcmp", "ksel", "kwin")
    outs_p = [stack(rows_p[i], n) for i, n in enumerate(names)]
    outs_s = [stack(rows_s[i], n) for i, n in enumerate(names)]
    result = [yp.reshape(bp, sp, D), ys.reshape(bs, ss, D)]
    for a, b in zip(outs_p, outs_s):
        result += [a, b]
    return tuple(result)
```
